```python
import jax
import jax.numpy as jnp
from jax import lax
import numpy as np

D_MODEL = 1024
BATCH = 8
SEQ = 4096
DEPTH = 1
DEC_BATCH = 128
DEC_SEQ = 1
PAST_LEN = 8192
PAGE_SIZE = 128

BRANCH_WIDTH = D_MODEL // 2
N_BRANCH = 3
CHUNK = 128
A_GROUPS = 8
A_GROUP_DIM = BRANCH_WIDTH // A_GROUPS
FOX_HEADS = 8
FOX_HEAD_DIM = BRANCH_WIDTH // FOX_HEADS
MEM_HEADS = 4
MEM_HEAD_DIM = BRANCH_WIDTH // MEM_HEADS
N_MEM = 256
Q_BLOCK = 128
NORM_EPS = 1e-6
CACHE_FORGET_LOGIT = 10.0
SPLIT_SIZES = (BRANCH_WIDTH,) * 7 + (FOX_HEADS,) + (BRANCH_WIDTH,) * 2 + (N_BRANCH * D_MODEL,)
IN_WIDTH = 9 * BRANCH_WIDTH + FOX_HEADS + N_BRANCH * D_MODEL

kernel_name = 'hybrid_gmlp_fox_memory_step'


def _rms_norm(x, g):
    xf = x.astype(jnp.float32)
    y = xf * lax.rsqrt(jnp.mean(xf * xf, axis=-1, keepdims=True) + NORM_EPS)
    return (y * g.astype(jnp.float32)).astype(x.dtype)


def _layer_norm(x, g, b):
    xf = x.astype(jnp.float32)
    mu = jnp.mean(xf, axis=-1, keepdims=True)
    var = jnp.mean(jnp.square(xf - mu), axis=-1, keepdims=True)
    y = (xf - mu) * lax.rsqrt(var + NORM_EPS) * g.astype(jnp.float32) + b.astype(jnp.float32)
    return y.astype(x.dtype)


def _heads(t, n_heads):
    return t.reshape(t.shape[0], t.shape[1], n_heads, -1)


def _project(h, w_in):
    z = jnp.einsum('bld,de->ble', h, w_in)
    points = []
    acc = 0
    for size in SPLIT_SIZES[:-1]:
        acc += size
        points.append(acc)
    return jnp.split(z, points, axis=-1)


def _chunk_spatial_gate(u, v, w_s, b_s):
    bsz, length, _ = v.shape
    n_chunks = -(-length // CHUNK)
    pad = n_chunks * CHUNK - length
    vc = jnp.pad(v, ((0, 0), (0, pad), (0, 0))).reshape(bsz, n_chunks, CHUNK, A_GROUPS, A_GROUP_DIM)
    mixed = jnp.einsum('gts,bnsgc->bntgc', jnp.tril(w_s), vc) + b_s.T[None, None, :, :, None]
    mixed = mixed.reshape(bsz, n_chunks * CHUNK, BRANCH_WIDTH)[:, :length]
    return u * mixed


def _chunk_branch(u_pre, v_pre, z, ln_g, ln_b, w_s, b_s):
    u = jax.nn.gelu(u_pre)
    v = _layer_norm(jax.nn.gelu(v_pre), ln_g, ln_b)
    return _chunk_spatial_gate(u, v, w_s, b_s) * jax.nn.silu(z), v


def _fox_prompt(q, k, v, logf):
    bsz, seq = q.shape[0], q.shape[1]
    scale = FOX_HEAD_DIM ** -0.5
    c = jnp.cumsum(logf, axis=1).transpose(0, 2, 1)
    n_blk = seq // Q_BLOCK
    qb = q.reshape(bsz, n_blk, Q_BLOCK, FOX_HEADS, FOX_HEAD_DIM).transpose(1, 0, 2, 3, 4)
    cb = c.reshape(bsz, FOX_HEADS, n_blk, Q_BLOCK).transpose(2, 0, 1, 3)
    key_pos = jnp.arange(seq)
    kf = k.astype(jnp.float32)

    def block(args):
        q_i, c_i, i = args
        s = jnp.einsum('bqhd,bkhd->bhqk', q_i.astype(jnp.float32), kf) * scale
        s = s + c_i[..., :, None] - c[:, :, None, :]
        q_pos = i * Q_BLOCK + jnp.arange(Q_BLOCK)
        s = jnp.where(key_pos[None, :] <= q_pos[:, None], s, -jnp.inf)
        p = jax.nn.softmax(s, axis=-1)
        return jnp.einsum('bhqk,bkhd->bqhd', p.astype(v.dtype), v)

    out = lax.map(block, (qb, cb, jnp.arange(n_blk)))
    return out.transpose(1, 0, 2, 3, 4).reshape(bsz, seq, BRANCH_WIDTH)


def _fox_decode(q, k, v, logf, cache_k, cache_v, cache_logf, page_table):
    f32 = jnp.float32
    bsz, length = q.shape[0], q.shape[1]
    n_pages = page_table.shape[1]
    scale = FOX_HEAD_DIM ** -0.5
    qf = q.astype(f32)
    c_new = jnp.cumsum(logf, axis=1).transpose(0, 2, 1)
    s0 = jnp.einsum('blhd,bshd->bhls', qf, k.astype(f32)) * scale
    s0 = s0 + c_new[..., :, None] - c_new[..., None, :]
    causal = jnp.tril(jnp.ones((length, length), dtype=bool))
    s0 = jnp.where(causal, s0, -jnp.inf)
    m0 = jnp.max(s0, axis=-1)
    p0 = jnp.exp(s0 - m0[..., None])
    carry0 = (m0, jnp.sum(p0, axis=-1), jnp.einsum('bhls,bshd->bhld', p0, v.astype(f32)))
    logf_past = cache_logf[page_table].reshape(bsz, n_pages * PAGE_SIZE, FOX_HEADS).astype(f32)
    suffix = jnp.cumsum(logf_past[:, ::-1], axis=1)[:, ::-1] - logf_past
    suffix_pages = suffix.reshape(bsz, n_pages, PAGE_SIZE, FOX_HEADS).transpose(1, 0, 3, 2)

    def page_step(carry, xs):
        m, l, acc = carry
        pages, r = xs
        kb = cache_k[pages].astype(f32)
        vb = cache_v[pages].astype(f32)
        s = jnp.einsum('blhd,bshd->bhls', qf, kb) * scale + c_new[..., :, None] + r[:, :, None, :]
        m_new = jnp.maximum(m, jnp.max(s, axis=-1))
        corr = jnp.exp(m - m_new)
        p = jnp.exp(s - m_new[..., None])
        acc = acc * corr[..., None] + jnp.einsum('bhls,bshd->bhld', p, vb)
        return (m_new, l * corr + jnp.sum(p, axis=-1), acc), None

    (_, l, acc), _ = lax.scan(page_step, carry0, (page_table.T, suffix_pages))
    out = acc / l[..., None]
    return out.transpose(0, 2, 1, 3).reshape(bsz, length, BRANCH_WIDTH).astype(q.dtype)


def _mem_kv(mem, g_mem, w_mem_kv):
    kv = jnp.einsum('bmd,de->bme', _rms_norm(mem, g_mem), w_mem_kv)
    k, v = jnp.split(kv, 2, axis=-1)
    return _heads(k, MEM_HEADS), _heads(v, MEM_HEADS)


def _mem_attend(q, k, v):
    qh = _heads(q, MEM_HEADS)
    s = jnp.einsum('blhd,bmhd->bhlm', qh.astype(jnp.float32), k.astype(jnp.float32)) * MEM_HEAD_DIM ** -0.5
    p = jax.nn.softmax(s, axis=-1)
    out = jnp.einsum('bhlm,bmhd->blhd', p.astype(v.dtype), v)
    return out.reshape(q.shape)


def _mixer_inputs(x, g_pre, w_in, b_forget, ln_g, ln_b, w_s, b_s):
    h = _rms_norm(x, g_pre)
    ua, va, za, qb, kb, vb, zb, fb, qc, zc, gates = _project(h, w_in)
    a_out, a_v = _chunk_branch(ua, va, za, ln_g, ln_b, w_s, b_s)
    logf = jax.nn.log_sigmoid(fb.astype(jnp.float32) + b_forget.astype(jnp.float32))
    return (a_out, a_v, _heads(qb, FOX_HEADS), _heads(kb, FOX_HEADS), _heads(vb, FOX_HEADS),
            zb, logf, qc, zc, gates)


def _finish(x, a_out, fox_out, zb, mem_out, zc, gates, w_branch, w_out, g_post):
    br = jnp.stack([a_out, fox_out * jax.nn.silu(zb), mem_out * jax.nn.silu(zc)], axis=2)
    proj = jnp.einsum('blnw,nwd->blnd', br, w_branch)
    g = jax.nn.sigmoid(gates.reshape(proj.shape))
    y = jnp.einsum('bld,de->ble', jnp.sum(g * proj, axis=2), w_out)
    return x + _rms_norm(y, g_post)


def setup_inputs(seed: int = 0) -> dict:
    key = jax.random.key(seed)
    ks = jax.random.split(key, 24)
    f32 = jnp.float32
    n_pages = PAST_LEN // PAGE_SIZE
    n_used = DEC_BATCH * n_pages
    n_phys = n_used + n_used // 4

    def normal(k, shape, scale=1.0):
        return jax.random.normal(k, shape, f32) * scale

    page_table = jax.random.permutation(ks[8], n_phys)[:n_used].reshape(DEC_BATCH, n_pages).astype(jnp.int32)
    return {
        'x_prompt': normal(ks[0], (BATCH, SEQ, D_MODEL)),
        'x_sample': normal(ks[1], (DEC_BATCH, DEC_SEQ, D_MODEL)),
        'mem_prompt': normal(ks[2], (BATCH, N_MEM, D_MODEL)),
        'cache_k': normal(ks[3], (DEPTH, n_phys, PAGE_SIZE, FOX_HEADS, FOX_HEAD_DIM)),
        'cache_v': normal(ks[4], (DEPTH, n_phys, PAGE_SIZE, FOX_HEADS, FOX_HEAD_DIM)),
        'cache_logf': jax.nn.log_sigmoid(CACHE_FORGET_LOGIT + normal(ks[5], (DEPTH, n_phys, PAGE_SIZE, FOX_HEADS))),
        'cache_mem_k': normal(ks[6], (DEPTH, DEC_BATCH, N_MEM, MEM_HEADS, MEM_HEAD_DIM)),
        'cache_mem_v': normal(ks[7], (DEPTH, DEC_BATCH, N_MEM, MEM_HEADS, MEM_HEAD_DIM)),
        'page_table': page_table,
        'g_pre': 1.0 + normal(ks[9], (DEPTH, D_MODEL), 0.05),
        'g_post': 1.0 + normal(ks[10], (DEPTH, D_MODEL), 0.05),
        'g_mem': 1.0 + normal(ks[11], (DEPTH, D_MODEL), 0.05),
        'w_in': normal(ks[12], (DEPTH, D_MODEL, IN_WIDTH), D_MODEL ** -0.5),
        'b_forget': 1.0 + 5.0 * jax.random.uniform(ks[13], (DEPTH, FOX_HEADS), f32),
        'ln_v_g': 1.0 + normal(ks[14], (DEPTH, BRANCH_WIDTH), 0.05),
        'ln_v_b': normal(ks[15], (DEPTH, BRANCH_WIDTH), 0.02),
        'w_spatial': normal(ks[16], (DEPTH, A_GROUPS, CHUNK, CHUNK), CHUNK ** -0.5),
        'b_spatial': 1.0 + normal(ks[17], (DEPTH, A_GROUPS, CHUNK), 0.1),
        'w_mem_kv': normal(ks[18], (DEPTH, D_MODEL, 2 * BRANCH_WIDTH), D_MODEL ** -0.5),
        'w_branch': normal(ks[19], (DEPTH, N_BRANCH, BRANCH_WIDTH, D_MODEL), BRANCH_WIDTH ** -0.5),
        'w_out': normal(ks[20], (DEPTH, D_MODEL, D_MODEL), D_MODEL ** -0.5),
    }


def reference(x_prompt, x_sample, mem_prompt, cache_k, cache_v, cache_logf, cache_mem_k, cache_mem_v,
              page_table, g_pre, g_post, g_mem, w_in, b_forget, ln_v_g, ln_v_b, w_spatial, b_spatial,
              w_mem_kv, w_branch, w_out):
    yp, ys = x_prompt, x_sample
    kp, vp, fp, mkp, mvp = [], [], [], [], []
    ksm, vsm, fsm, avs = [], [], [], []
    for layer in range(DEPTH):
        lw = (g_pre[layer], w_in[layer], b_forget[layer], ln_v_g[layer], ln_v_b[layer],
              w_spatial[layer], b_spatial[layer])
        a_out, _, q, k, v, zb, logf, qc, zc, gates = _mixer_inputs(yp, *lw)
        fox = _fox_prompt(q, k, v, logf)
        mk, mv = _mem_kv(mem_prompt, g_mem[layer], w_mem_kv[layer])
        mem_out = _mem_attend(qc, mk, mv)
        yp = _finish(yp, a_out, fox, zb, mem_out, zc, gates, w_branch[layer], w_out[layer], g_post[layer])
        kp.append(k)
        vp.append(v)
        fp.append(logf)
        mkp.append(mk)
        mvp.append(mv)
        a_out, a_v, q, k, v, zb, logf, qc, zc, gates = _mixer_inputs(ys, *lw)
        fox = _fox_decode(q, k, v, logf, cache_k[layer], cache_v[layer], cache_logf[layer], page_table)
        mem_out = _mem_attend(qc, cache_mem_k[layer], cache_mem_v[layer])
        ys = _finish(ys, a_out, fox, zb, mem_out, zc, gates, w_branch[layer], w_out[layer], g_post[layer])
        ksm.append(k)
        vsm.append(v)
        fsm.append(logf)
        avs.append(a_v)
    return (yp, ys, jnp.stack(kp), jnp.stack(vp), jnp.stack(fp), jnp.stack(mkp), jnp.stack(mvp),
            jnp.stack(ksm), jnp.stack(vsm), jnp.stack(fsm), jnp.stack(avs))
```

```python
import functools

import jax
import jax.numpy as jnp
from jax import lax
from jax.experimental import pallas as pl
from jax.experimental.pallas import tpu as pltpu

F32 = jnp.float32
BF16 = jnp.bfloat16

D_MODEL = 1024
WIDTH = 512
FOX_HEADS = 8
FOX_DIM = 64
MEM_HEADS = 4
MEM_DIM = 128
N_MEM = 256
CHUNK = 128
PAGE = 128
NORM_EPS = 1e-6
FOX_SCALE = FOX_DIM ** -0.5
MEM_SCALE = MEM_DIM ** -0.5
LANES = 128

R_UA, R_VA, R_ZA, R_ZB, R_QC, R_ZC, R_GATE = (i * WIDTH for i in range(7))
REST_WIDTH = R_GATE + 3 * D_MODEL

QKV_TILE = 512
ATTN_TILE = 256
FINISH_TILE = 256
RING = 8
VMEM_LIMIT = 56 * 1024 * 1024

_NT = (((1,), (1,)), ((), ()))


def _dot(a, b):
    return jnp.dot(a, b, preferred_element_type=F32)


def _dot_nt(a, b):
    return lax.dot_general(a, b, _NT, preferred_element_type=F32)


def _rms(x, g):
    ms = jnp.mean(x * x, axis=-1, keepdims=True)
    return x * lax.rsqrt(ms + NORM_EPS) * g


def _sigmoid(x):
    return 1.0 / (1.0 + jnp.exp(-x))


def _silu(x):
    return x * _sigmoid(x)


def _gelu(x):
    return jax.nn.gelu(x)


def _log_sigmoid(x):
    return jnp.minimum(x, 0.0) - jnp.log1p(jnp.exp(-jnp.abs(x)))


def _layer_norm(x, g, b):
    mu = jnp.mean(x, axis=-1, keepdims=True)
    xc = x - mu
    var = jnp.mean(xc * xc, axis=-1, keepdims=True)
    return xc * lax.rsqrt(var + NORM_EPS) * g + b


def _split3(x):
    hi = x.astype(BF16)
    r = x - hi.astype(F32)
    mid = r.astype(BF16)
    lo = (r - mid.astype(F32)).astype(BF16)
    return hi, mid, lo


def _params(n_axes):
    return pltpu.CompilerParams(dimension_semantics=("arbitrary",) * n_axes,
                                vmem_limit_bytes=VMEM_LIMIT)


def _resident(shape):
    return pl.BlockSpec(shape, lambda *_: (0,) * len(shape), pipeline_mode=pl.Buffered(1))


def _qkv_kernel(x_ref, g_ref, w_ref, wft_ref, bf_ref, q_ref, kb_ref, vb_ref, k_ref, v_ref, lft_ref):
    h = _rms(x_ref[...], g_ref[...]).astype(BF16)
    z = _dot(h, w_ref[...])
    q_ref[...] = (z[:, :WIDTH] * FOX_SCALE).astype(BF16)
    k = z[:, WIDTH:2 * WIDTH]
    v = z[:, 2 * WIDTH:]
    k_ref[...] = k
    v_ref[...] = v
    kb_ref[...] = k.astype(BF16)
    vb_ref[...] = v.astype(BF16)
    ft = _dot_nt(wft_ref[...], h)
    lft_ref[0] = _log_sigmoid(ft[:FOX_HEADS] + bf_ref[...])


def _qkv_project(x2d, n_seq, seq_len, g_pre, w_qkv, w_ft, b_f, tm):
    m = x2d.shape[0]
    per_seq = seq_len // tm
    row = lambda i: (i, 0)
    return pl.pallas_call(
        _qkv_kernel,
        grid=(m // tm,),
        in_specs=[pl.BlockSpec((tm, D_MODEL), row),
                  _resident((1, D_MODEL)),
                  _resident((D_MODEL, 3 * WIDTH)),
                  _resident((16, D_MODEL)),
                  _resident((FOX_HEADS, 1))],
        out_specs=[pl.BlockSpec((tm, WIDTH), row)] * 5
        + [pl.BlockSpec((1, FOX_HEADS, tm), lambda i: (i // per_seq, 0, i % per_seq))],
        out_shape=[jax.ShapeDtypeStruct((m, WIDTH), BF16)] * 3
        + [jax.ShapeDtypeStruct((m, WIDTH), F32)] * 2
        + [jax.ShapeDtypeStruct((n_seq, FOX_HEADS, seq_len), F32)],
        compiler_params=_params(1),
        name="qkv_project",
    )(x2d, g_pre, w_qkv, w_ft, b_f)


def _neg_cumsum_kernel(lft_ref, o_ref):
    x = lft_ref[0]
    pos = lax.broadcasted_iota(jnp.int32, x.shape, 1)
    shift = 1
    while shift < x.shape[-1]:
        x = x + jnp.where(pos >= shift, pltpu.roll(x, shift, 1), 0.0)
        shift *= 2
    o_ref[0] = -x


def _neg_cumsum(lft):
    n_seq, heads, seq_len = lft.shape
    blk = pl.BlockSpec((1, heads, seq_len), lambda b: (b, 0, 0))
    return pl.pallas_call(
        _neg_cumsum_kernel, grid=(n_seq,), in_specs=[blk], out_specs=blk,
        out_shape=jax.ShapeDtypeStruct(lft.shape, F32),
        compiler_params=_params(1), name="decay_cumsum",
    )(lft)


def _fox_kernel(q_ref, k_ref, v_ref, nc_ref, o_ref, *, t):
    qi = pl.program_id(1)
    row = lax.broadcasted_iota(jnp.int32, (t, t), 0)
    col = lax.broadcasted_iota(jnp.int32, (t, t), 1)
    causal = col <= row
    lower = lax.broadcasted_iota(jnp.int32, (t, LANES), 1) < FOX_DIM

    for j in range(WIDTH // LANES):
        lanes = slice(j * LANES, (j + 1) * LANES)
        q_blk = q_ref[0, :, lanes]
        outs = []
        for par in range(2):
            h = 2 * j + par
            q_h = jnp.where(lower if par == 0 else jnp.logical_not(lower), q_blk, jnp.zeros_like(q_blk))

            def step(kv, carry, masked, q_h=q_h, h=h, lanes=lanes):
                m, l, acc = carry
                off = pl.multiple_of(kv * t, t)
                k_blk = k_ref[0, pl.ds(off, t), lanes]
                v_blk = v_ref[0, pl.ds(off, t), lanes]
                s = _dot_nt(q_h, k_blk) + nc_ref[0, h:h + 1, pl.ds(off, t)]
                if masked:
                    s = jnp.where(causal, s, -jnp.inf)
                m_new = jnp.maximum(m, jnp.max(s, axis=-1, keepdims=True))
                p = jnp.exp(s - m_new)
                corr = jnp.exp(m - m_new)
                l = l * corr + jnp.sum(p, axis=-1, keepdims=True)
                acc = acc * corr + _dot(p.astype(BF16), v_blk)
                return m_new, l, acc

            init = (jnp.full((t, 1), -jnp.inf, F32), jnp.zeros((t, 1), F32), jnp.zeros((t, LANES), F32))
            carry = lax.fori_loop(0, qi, functools.partial(step, masked=False), init)
            _, l, acc = step(qi, carry, masked=True)
            outs.append(acc / l)
        o_ref[0, :, lanes] = jnp.where(lower, outs[0], outs[1])


def _fox_attention(q, k, v, neg_c, t):
    n_seq, seq_len, _ = q.shape
    tile = pl.BlockSpec((1, t, WIDTH), lambda b, i: (b, i, 0))
    whole = pl.BlockSpec((1, seq_len, WIDTH), lambda b, i: (b, 0, 0))
    return pl.pallas_call(
        functools.partial(_fox_kernel, t=t),
        grid=(n_seq, seq_len // t),
        in_specs=[tile, whole, whole,
                  pl.BlockSpec((1, FOX_HEADS, seq_len), lambda b, i: (b, 0, 0))],
        out_specs=tile,
        out_shape=jax.ShapeDtypeStruct((n_seq, seq_len, WIDTH), F32),
        compiler_params=_params(2), name="fox_attention",
    )(q, k, v, neg_c)


def _mem_kv_kernel(mem_ref, g_ref, w_ref, k_ref, v_ref, kb_ref, vb_ref):
    h = _rms(mem_ref[0], g_ref[...]).astype(BF16)
    kv = _dot(h, w_ref[...])
    k_ref[0] = kv[:, :WIDTH]
    v_ref[0] = kv[:, WIDTH:]
    kb_ref[0] = kv[:, :WIDTH].astype(BF16)
    vb_ref[0] = kv[:, WIDTH:].astype(BF16)


def _mem_kv(mem, g_mem, w_mem_kv):
    n_seq = mem.shape[0]
    blk = pl.BlockSpec((1, N_MEM, WIDTH), lambda b: (b, 0, 0))
    return pl.pallas_call(
        _mem_kv_kernel, grid=(n_seq,),
        in_specs=[pl.BlockSpec((1, N_MEM, D_MODEL), lambda b: (b, 0, 0)),
                  _resident((1, D_MODEL)), _resident((D_MODEL, 2 * WIDTH))],
        out_specs=[blk] * 4,
        out_shape=[jax.ShapeDtypeStruct((n_seq, N_MEM, WIDTH), F32)] * 2
        + [jax.ShapeDtypeStruct((n_seq, N_MEM, WIDTH), BF16)] * 2,
        compiler_params=_params(1), name="mem_kv",
    )(mem, g_mem, w_mem_kv)


def _gated_merge(h_or_gates, branch_ref, wbr_ref, wout_ref, gate_fn):
    merged = None
    for n in range(3):
        proj = _dot(branch_ref[n], wbr_ref[n])
        term = _sigmoid(gate_fn(n)) * proj
        merged = term if merged is None else merged + term
    return _dot(merged.astype(BF16), wout_ref[...])


def _finish_kernel(x_ref, fox_ref, gpre_ref, w_ref, lng_ref, lnb_ref, wsp_ref, bsp_ref,
                   mk_ref, mv_ref, wbr_ref, wout_ref, gpost_ref, y_ref, br_ref, *, tm):
    x = x_ref[...]
    h = _rms(x, gpre_ref[...]).astype(BF16)

    def proj(c0, width=WIDTH):
        return _dot(h, w_ref[:, c0:c0 + width])

    u = _gelu(proj(R_UA))
    v = _layer_norm(_gelu(proj(R_VA)), lng_ref[...], lnb_ref[...]).astype(BF16)
    gate_a = _silu(proj(R_ZA))
    n_chunk = tm // CHUNK
    r2 = lax.broadcasted_iota(jnp.int32, (2 * CHUNK, CHUNK), 0)
    c2 = lax.broadcasted_iota(jnp.int32, (2 * CHUNK, CHUNK), 1)
    tril = (r2 & (CHUNK - 1)) >= c2
    lower = lax.broadcasted_iota(jnp.int32, (CHUNK, LANES), 1) < (LANES // 2)
    for j in range(WIDTH // LANES):
        lanes = slice(j * LANES, (j + 1) * LANES)
        w_pair = jnp.where(tril, wsp_ref[j], 0.0).astype(BF16)
        rhs = jnp.concatenate([v[c * CHUNK:(c + 1) * CHUNK, lanes] for c in range(n_chunk)], axis=1)
        mix = _dot(w_pair, rhs)
        for c in range(n_chunk):
            blk = mix[:, c * LANES:(c + 1) * LANES]
            mixed = jnp.where(lower, blk[:CHUNK], blk[CHUNK:]) + bsp_ref[:, lanes]
            rows = slice(c * CHUNK, (c + 1) * CHUNK)
            br_ref[0, rows, lanes] = (u[rows, lanes] * mixed * gate_a[rows, lanes]).astype(BF16)

    br_ref[1] = (fox_ref[...] * _silu(proj(R_ZB))).astype(BF16)

    qc = proj(R_QC)
    gate_c = _silu(proj(R_ZC))
    for hh in range(MEM_HEADS):
        lanes = slice(hh * MEM_DIM, (hh + 1) * MEM_DIM)
        s = _dot_nt(qc[:, lanes].astype(BF16), mk_ref[0, :, lanes]) * MEM_SCALE
        p = jnp.exp(s - jnp.max(s, axis=-1, keepdims=True))
        o = _dot(p.astype(BF16), mv_ref[0, :, lanes]) / jnp.sum(p, axis=-1, keepdims=True)
        br_ref[2, :, lanes] = (o * gate_c[:, lanes]).astype(BF16)

    y = _gated_merge(h, br_ref, wbr_ref, wout_ref, lambda n: proj(R_GATE + n * D_MODEL, D_MODEL))
    y_ref[...] = x + _rms(y, gpost_ref[...])


def _finish(x2d, fox2d, seq_len, g_pre, w_rest, ln_g, ln_b, w_sp, b_sp, mkb, mvb, w_br, w_out, g_post, tm):
    m = x2d.shape[0]
    per_seq = seq_len // tm
    row = lambda i: (i, 0)
    mem = pl.BlockSpec((1, N_MEM, WIDTH), lambda i: (i // per_seq, 0, 0))
    return pl.pallas_call(
        functools.partial(_finish_kernel, tm=tm),
        grid=(m // tm,),
        in_specs=[pl.BlockSpec((tm, D_MODEL), row), pl.BlockSpec((tm, WIDTH), row),
                  _resident((1, D_MODEL)), _resident((D_MODEL, REST_WIDTH)),
                  _resident((1, WIDTH)), _resident((1, WIDTH)),
                  _resident((WIDTH // LANES, 2 * CHUNK, CHUNK)), _resident((CHUNK, WIDTH)),
                  mem, mem,
                  _resident((3, WIDTH, D_MODEL)), _resident((D_MODEL, D_MODEL)), _resident((1, D_MODEL))],
        out_specs=pl.BlockSpec((tm, D_MODEL), row),
        out_shape=jax.ShapeDtypeStruct((m, D_MODEL), F32),
        scratch_shapes=[pltpu.VMEM((3, tm, WIDTH), BF16)],
        compiler_params=_params(1), name="prompt_finish",
    )(x2d, fox2d, g_pre, w_rest, ln_g, ln_b, w_sp, b_sp, mkb, mvb, w_br, w_out, g_post)


def _norm_proj_kernel(x_ref, g_ref, w_ref, z_ref):
    z_ref[...] = _dot(_rms(x_ref[...], g_ref[...]).astype(BF16), w_ref[...])


def _norm_project(x2d, g_pre, w, tn):
    m, n = x2d.shape[0], w.shape[1]
    return pl.pallas_call(
        _norm_proj_kernel, grid=(n // tn,),
        in_specs=[_resident((m, D_MODEL)), _resident((1, D_MODEL)),
                  pl.BlockSpec((D_MODEL, tn), lambda j: (0, j))],
        out_specs=pl.BlockSpec((m, tn), lambda j: (0, j)),
        out_shape=jax.ShapeDtypeStruct((m, n), F32),
        compiler_params=_params(1), name="sample_rest_project",
    )(x2d, g_pre, w)


def _head_rows(row_vec, n_rows, head_dim):
    width = row_vec.shape[-1]
    r = lax.broadcasted_iota(jnp.int32, (n_rows, width), 0)
    c = lax.broadcasted_iota(jnp.int32, (n_rows, width), 1)
    mask = (c // head_dim) == r
    return jnp.where(mask, jnp.broadcast_to(row_vec, (n_rows, width)), 0.0), mask


def _decode_kernel(pt_ref, q_ref, kn_ref, vn_ref, cn_ref, ck_ref, cv_ref, clf_ref, o_ref,
                   ring, lfbuf, sc, sem_ring, sem_lf, *, n_pages, n_seq):
    b = pl.program_id(0)
    n_grp = n_pages // RING

    def page_copy(src, seq, p, slot):
        return pltpu.make_async_copy(src.at[pt_ref[seq, p]], ring.at[slot], sem_ring.at[slot])

    def lf_copy(seq, p, half):
        return pltpu.make_async_copy(clf_ref.at[pt_ref[seq, p]], lfbuf.at[half, p], sem_lf.at[half])

    @pl.when(b == 0)
    def _():
        for p in range(n_pages):
            lf_copy(0, p, 0).start()
        for u in range(RING):
            page_copy(ck_ref, 0, u, u).start()

    @pl.when(b + 1 < n_seq)
    def _():
        for p in range(n_pages):
            lf_copy(b + 1, p, (b + 1) % 2).start()

    half = b % 2
    for p in range(n_pages):
        lf_copy(b, p, half).wait()

    lf = lfbuf[half].reshape(n_pages * FOX_HEADS, PAGE)
    r = lax.broadcasted_iota(jnp.int32, (PAGE, PAGE), 0)
    c = lax.broadcasted_iota(jnp.int32, (PAGE, PAGE), 1)
    after = (r > c).astype(BF16)
    ones = jnp.ones((PAGE, PAGE), BF16)
    parts = _split3(lf)
    within = sum(_dot(x, after) for x in parts).reshape(n_pages, FOX_HEADS, PAGE)
    total = sum(_dot(x, ones) for x in parts).reshape(n_pages, FOX_HEADS, PAGE)
    run = jnp.broadcast_to(cn_ref[0], (FOX_HEADS, PAGE))
    for p in reversed(range(n_pages)):
        sc[p] = within[p] + run
        run = run + total[p]

    q_f32, head_mask = _head_rows(q_ref[0].astype(F32), FOX_HEADS, FOX_DIM)
    q_rows = q_f32.astype(BF16)

    def k_group(grp, prefetch):
        for u in range(RING):
            p = grp * RING + u
            page_copy(ck_ref, b, p, u).wait()
            sc[p] = sc[p] + _dot_nt(q_rows, ring[u].astype(BF16))
            prefetch(p, u)

    def k_body(grp, carry):
        k_group(grp, lambda p, u: page_copy(ck_ref, b, p + RING, u).start())
        return carry

    lax.fori_loop(0, n_grp - 1, k_body, 0)
    k_group(n_grp - 1, lambda p, u: page_copy(cv_ref, b, u, u).start())

    s_all = sc[...]
    s_new = jnp.sum(q_f32 * kn_ref[0], axis=-1, keepdims=True)
    m = jnp.maximum(jnp.max(jnp.max(s_all, axis=0), axis=-1, keepdims=True), s_new)
    p_all = jnp.exp(s_all - m[None])
    p_new = jnp.exp(s_new - m)
    denom = jnp.sum(jnp.sum(p_all, axis=0), axis=-1, keepdims=True) + p_new
    sc[...] = p_all

    def v_group(grp, acc, prefetch):
        for u in range(RING):
            p = grp * RING + u
            page_copy(cv_ref, b, p, u).wait()
            acc = acc + _dot(sc[p].astype(BF16), ring[u].astype(BF16))
            prefetch(p, u)
        return acc

    def v_body(grp, acc):
        return v_group(grp, acc, lambda p, u: page_copy(cv_ref, b, p + RING, u).start())

    acc = lax.fori_loop(0, n_grp - 1, v_body, jnp.zeros((FOX_HEADS, WIDTH), F32))

    def next_seq(p, u):
        @pl.when(b + 1 < n_seq)
        def _():
            page_copy(ck_ref, b + 1, u, u).start()

    acc = v_group(n_grp - 1, acc, next_seq)
    acc = acc + p_new * vn_ref[0]
    o_ref[0] = jnp.sum(jnp.where(head_mask, acc, 0.0) / denom, axis=0, keepdims=True)


def _fox_decode(page_table, q, k_new, v_new, c_new, cache_k, cache_v, cache_lft):
    n_seq, n_pages = page_table.shape
    assert n_pages % RING == 0 and n_pages >= 2 * RING
    per_seq = lambda shape: pl.BlockSpec((1,) + shape, lambda b, pt: (b, 0, 0))
    hbm = pl.BlockSpec(memory_space=pl.ANY)
    grid_spec = pltpu.PrefetchScalarGridSpec(
        num_scalar_prefetch=1, grid=(n_seq,),
        in_specs=[per_seq((1, WIDTH)), per_seq((1, WIDTH)), per_seq((1, WIDTH)), per_seq((FOX_HEADS, 1)),
                  hbm, hbm, hbm],
        out_specs=per_seq((1, WIDTH)),
        scratch_shapes=[pltpu.VMEM((RING, PAGE, WIDTH), F32),
                        pltpu.VMEM((2, n_pages, FOX_HEADS, PAGE), F32),
                        pltpu.VMEM((n_pages, FOX_HEADS, PAGE), F32),
                        pltpu.SemaphoreType.DMA((RING,)),
                        pltpu.SemaphoreType.DMA((2,))])
    return pl.pallas_call(
        functools.partial(_decode_kernel, n_pages=n_pages, n_seq=n_seq),
        grid_spec=grid_spec,
        out_shape=jax.ShapeDtypeStruct((n_seq, 1, WIDTH), F32),
        compiler_params=_params(1), name="fox_decode",
    )(page_table, q, k_new, v_new, c_new, cache_k, cache_v, cache_lft)


def _mem_decode_kernel(q_ref, k_ref, v_ref, o_ref):
    q_rows, head_mask = _head_rows(q_ref[0], 2 * MEM_HEADS, MEM_DIM)
    s = _dot_nt(q_rows.astype(BF16), k_ref[0].astype(BF16)) * MEM_SCALE
    p = jnp.exp(s - jnp.max(s, axis=-1, keepdims=True))
    o = _dot(p.astype(BF16), v_ref[0].astype(BF16)) / jnp.sum(p, axis=-1, keepdims=True)
    o_ref[0] = jnp.sum(jnp.where(head_mask, o, 0.0), axis=0, keepdims=True)


def _mem_decode(q, mem_k, mem_v):
    n_seq = q.shape[0]
    tok = pl.BlockSpec((1, 1, WIDTH), lambda b: (b, 0, 0))
    mem = pl.BlockSpec((1, N_MEM, WIDTH), lambda b: (b, 0, 0))
    return pl.pallas_call(
        _mem_decode_kernel, grid=(n_seq,), in_specs=[tok, mem, mem], out_specs=tok,
        out_shape=jax.ShapeDtypeStruct((n_seq, 1, WIDTH), F32),
        compiler_params=_params(1), name="mem_decode",
    )(q, mem_k, mem_v)


def _sample_finish_kernel(x_ref, z_ref, fox_ref, mem_ref, lng_ref, lnb_ref, ws0_ref, bs0_ref,
                          wbr_ref, wout_ref, gpost_ref, y_ref, av_ref, br_ref):
    col = lambda c0, width=WIDTH: z_ref[:, c0:c0 + width]
    v = _layer_norm(_gelu(col(R_VA)), lng_ref[...], lnb_ref[...])
    av_ref[...] = v
    mixed = ws0_ref[...] * v + bs0_ref[...]
    br_ref[0] = (_gelu(col(R_UA)) * mixed * _silu(col(R_ZA))).astype(BF16)
    br_ref[1] = (fox_ref[...] * _silu(col(R_ZB))).astype(BF16)
    br_ref[2] = (mem_ref[...] * _silu(col(R_ZC))).astype(BF16)
    y = _gated_merge(None, br_ref, wbr_ref, wout_ref, lambda n: col(R_GATE + n * D_MODEL, D_MODEL))
    y_ref[...] = x_ref[...] + _rms(y, gpost_ref[...])


def _sample_finish(x2d, z, fox, mem, ln_g, ln_b, ws0, bs0, w_br, w_out, g_post):
    m = x2d.shape[0]
    args = (x2d, z, fox, mem, ln_g, ln_b, ws0, bs0, w_br, w_out, g_post)
    return pl.pallas_call(
        _sample_finish_kernel, grid=(1,),
        in_specs=[_resident(a.shape) for a in args],
        out_specs=[pl.BlockSpec((m, D_MODEL), lambda i: (0, 0)), pl.BlockSpec((m, WIDTH), lambda i: (0, 0))],
        out_shape=[jax.ShapeDtypeStruct((m, D_MODEL), F32), jax.ShapeDtypeStruct((m, WIDTH), F32)],
        scratch_shapes=[pltpu.VMEM((3, m, WIDTH), BF16)],
        compiler_params=_params(1), name="sample_finish",
    )(*args)


def kernel(x_prompt, x_sample, mem_prompt, cache_k, cache_v, cache_logf, cache_mem_k, cache_mem_v,
           page_table, g_pre, g_post, g_mem, w_in, b_forget, ln_v_g, ln_v_b, w_spatial, b_spatial,
           w_mem_kv, w_branch, w_out):
    depth = g_pre.shape[0]
    assert depth == 1
    n_seq, seq_len, _ = x_prompt.shape
    n_dec, dec_len, _ = x_sample.shape
    assert dec_len == 1
    n_phys = cache_k.shape[1]

    w = w_in[0]
    o_f = 7 * WIDTH
    w_qkv = w[:, 3 * WIDTH:6 * WIDTH].astype(BF16)
    w_ft = jnp.zeros((16, D_MODEL), F32).at[:FOX_HEADS].set(w[:, o_f:o_f + FOX_HEADS].T).astype(BF16)
    w_rest = jnp.concatenate([w[:, :3 * WIDTH], w[:, 6 * WIDTH:o_f], w[:, o_f + FOX_HEADS:]], axis=1).astype(BF16)
    b_f = b_forget[0].reshape(FOX_HEADS, 1)
    gpre, gpost, gmem = g_pre[0][None], g_post[0][None], g_mem[0][None]
    ln_g, ln_b = ln_v_g[0][None], ln_v_b[0][None]
    w_sp = w_spatial[0].reshape(WIDTH // LANES, 2 * CHUNK, CHUNK)
    b_sp = jnp.repeat(b_spatial[0].T, WIDTH // 8, axis=1)
    ws0 = jnp.repeat(w_spatial[0][:, 0, 0], WIDTH // 8)[None]
    bs0 = b_sp[:1]
    w_br = w_branch[0].astype(BF16)
    w_o = w_out[0].astype(BF16)
    w_mkv = w_mem_kv[0].astype(BF16)

    m = n_seq * seq_len
    xp = x_prompt.reshape(m, D_MODEL)
    tq = min(QKV_TILE, seq_len)
    q, kb, vb, k, v, lft = _qkv_project(xp, n_seq, seq_len, gpre, w_qkv, w_ft, b_f, tq)
    neg_c = _neg_cumsum(lft)
    t_attn = min(ATTN_TILE, seq_len)
    fox = _fox_attention(q.reshape(n_seq, seq_len, WIDTH), kb.reshape(n_seq, seq_len, WIDTH),
                         vb.reshape(n_seq, seq_len, WIDTH), neg_c, t_attn)
    mk, mv, mkb, mvb = _mem_kv(mem_prompt, gmem, w_mkv)
    yp = _finish(xp, fox.reshape(m, WIDTH), seq_len, gpre, w_rest, ln_g, ln_b, w_sp, b_sp, mkb, mvb,
                 w_br, w_o, gpost, min(FINISH_TILE, seq_len))

    xs = x_sample.reshape(n_dec, D_MODEL)
    qs, _, _, ks, vs, lfts = _qkv_project(xs, 1, n_dec, gpre, w_qkv, w_ft, b_f, n_dec)
    logf_s = lfts[0].T
    zs = _norm_project(xs, gpre, w_rest, REST_WIDTH // 4)
    fox_s = _fox_decode(page_table, qs.reshape(n_dec, 1, WIDTH), ks.reshape(n_dec, 1, WIDTH),
                        vs.reshape(n_dec, 1, WIDTH), logf_s.reshape(n_dec, FOX_HEADS, 1),
                        cache_k[0].reshape(n_phys, PAGE, WIDTH), cache_v[0].reshape(n_phys, PAGE, WIDTH),
                        cache_logf[0].transpose(0, 2, 1))
    mem_s = _mem_decode(zs[:, R_QC:R_QC + WIDTH].reshape(n_dec, 1, WIDTH),
                        cache_mem_k[0].reshape(n_dec, N_MEM, WIDTH), cache_mem_v[0].reshape(n_dec, N_MEM, WIDTH))
    ys, av = _sample_finish(xs, zs, fox_s.reshape(n_dec, WIDTH), mem_s.reshape(n_dec, WIDTH),
                            ln_g, ln_b, ws0, bs0, w_br, w_o, gpost)

    heads = (FOX_HEADS, FOX_DIM)
    return (yp.reshape(n_seq, seq_len, D_MODEL),
            ys.reshape(n_dec, 1, D_MODEL),
            k.reshape(1, n_seq, seq_len, *heads), v.reshape(1, n_seq, seq_len, *heads),
            lft.transpose(0, 2, 1)[None],
            mk.reshape(1, n_seq, N_MEM, MEM_HEADS, MEM_DIM), mv.reshape(1, n_seq, N_MEM, MEM_HEADS, MEM_DIM),
            ks.reshape(1, n_dec, 1, *heads), vs.reshape(1, n_dec, 1, *heads),
            logf_s.reshape(1, n_dec, 1, FOX_HEADS),
            av.reshape(1, n_dec, 1, WIDTH))
```

```python
import functools

import jax
import jax.numpy as jnp
from jax import lax
from jax.experimental import pallas as pl
from jax.experimental.pallas import tpu as pltpu

F32 = jnp.float32
BF16 = jnp.bfloat16

D_MODEL = 1024
WIDTH = 512
FOX_HEADS = 8
FOX_DIM = 64
MEM_HEADS = 4
MEM_DIM = 128
N_MEM = 256
CHUNK = 128
PAGE = 128
NORM_EPS = 1e-6
FOX_SCALE = FOX_DIM ** -0.5
MEM_SCALE = MEM_DIM ** -0.5
LANES = 128

R_UA, R_VA, R_ZA, R_ZB, R_QC, R_ZC, R_GATE = (i * WIDTH for i in range(7))
REST_WIDTH = R_GATE + 3 * D_MODEL

QKV_TILE = 512
ATTN_TILE = 512
FINISH_TILE = 256
GROUP = 8
RING = 2 * GROUP
VMEM_LIMIT = 56 * 1024 * 1024

_NT = (((1,), (1,)), ((), ()))


def _dot(a, b):
    return jnp.dot(a, b, preferred_element_type=F32)


def _dot_nt(a, b):
    return lax.dot_general(a, b, _NT, preferred_element_type=F32)


def _rms(x, g):
    ms = jnp.mean(x * x, axis=-1, keepdims=True)
    return x * lax.rsqrt(ms + NORM_EPS) * g


def _sigmoid(x):
    return 1.0 / (1.0 + jnp.exp(-x))


def _silu(x):
    return x * _sigmoid(x)


def _gelu(x):
    return jax.nn.gelu(x)


def _log_sigmoid(x):
    return jnp.minimum(x, 0.0) - jnp.log1p(jnp.exp(-jnp.abs(x)))


def _layer_norm(x, g, b):
    mu = jnp.mean(x, axis=-1, keepdims=True)
    xc = x - mu
    var = jnp.mean(xc * xc, axis=-1, keepdims=True)
    return xc * lax.rsqrt(var + NORM_EPS) * g + b


def _split3(x):
    hi = x.astype(BF16)
    r = x - hi.astype(F32)
    mid = r.astype(BF16)
    lo = (r - mid.astype(F32)).astype(BF16)
    return hi, mid, lo


def _params(n_axes):
    return pltpu.CompilerParams(dimension_semantics=("arbitrary",) * n_axes,
                                vmem_limit_bytes=VMEM_LIMIT)


def _resident(shape):
    return pl.BlockSpec(shape, lambda *_: (0,) * len(shape), pipeline_mode=pl.Buffered(1))


def _qkv_kernel(x_ref, g_ref, w_ref, wkvt_ref, wft_ref, bf_ref, q_ref, kb_ref, vtb_ref, kt_ref, vt_ref, lft_ref):
    h = _rms(x_ref[...], g_ref[...]).astype(BF16)
    z = _dot(h, w_ref[...])
    q_ref[...] = (z[:, :WIDTH] * FOX_SCALE).astype(BF16)
    kb_ref[...] = z[:, WIDTH:].astype(BF16)
    zt = _dot_nt(wkvt_ref[...], h)
    kt_ref[0] = zt[:WIDTH]
    vt_ref[0] = zt[WIDTH:]
    vtb_ref[0] = zt[WIDTH:].astype(BF16)
    ft = _dot_nt(wft_ref[...], h)
    lft_ref[0] = _log_sigmoid(ft[:FOX_HEADS] + bf_ref[...])


def _qkv_project(x2d, n_seq, seq_len, g_pre, w_qk, w_kvt, w_ft, b_f, tm):
    m = x2d.shape[0]
    per_seq = seq_len // tm
    row = lambda i: (i, 0)
    chan = lambda rows: pl.BlockSpec((1, rows, tm), lambda i: (i // per_seq, 0, i % per_seq))
    return pl.pallas_call(
        _qkv_kernel,
        grid=(m // tm,),
        in_specs=[pl.BlockSpec((tm, D_MODEL), row),
                  _resident((1, D_MODEL)),
                  _resident((D_MODEL, 2 * WIDTH)),
                  _resident((2 * WIDTH, D_MODEL)),
                  _resident((16, D_MODEL)),
                  _resident((FOX_HEADS, 1))],
        out_specs=[pl.BlockSpec((tm, WIDTH), row)] * 2 + [chan(WIDTH)] * 3 + [chan(FOX_HEADS)],
        out_shape=[jax.ShapeDtypeStruct((m, WIDTH), BF16)] * 2
        + [jax.ShapeDtypeStruct((n_seq, WIDTH, seq_len), BF16)]
        + [jax.ShapeDtypeStruct((n_seq, WIDTH, seq_len), F32)] * 2
        + [jax.ShapeDtypeStruct((n_seq, FOX_HEADS, seq_len), F32)],
        compiler_params=_params(1),
        name="qkv_project",
    )(x2d, g_pre, w_qk, w_kvt, w_ft, b_f)


def _neg_cumsum_kernel(lft_ref, o_ref):
    x = lft_ref[0]
    pos = lax.broadcasted_iota(jnp.int32, x.shape, 1)
    shift = 1
    while shift < x.shape[-1]:
        x = x + jnp.where(pos >= shift, pltpu.roll(x, shift, 1), 0.0)
        shift *= 2
    hi, mid, lo = _split3(-x)
    o_ref[0] = jnp.concatenate([hi.astype(F32), mid.astype(F32), lo.astype(F32)], axis=0)


def _neg_cumsum(lft):
    n_seq, heads, seq_len = lft.shape
    return pl.pallas_call(
        _neg_cumsum_kernel, grid=(n_seq,),
        in_specs=[pl.BlockSpec((1, heads, seq_len), lambda b: (b, 0, 0))],
        out_specs=pl.BlockSpec((1, 3 * heads, seq_len), lambda b: (b, 0, 0)),
        out_shape=jax.ShapeDtypeStruct((n_seq, 3 * heads, seq_len), F32),
        compiler_params=_params(1), name="decay_cumsum",
    )(lft)


def _fox_kernel(q_ref, k_ref, vt_ref, nc_ref, o_ref, m_ref, l_ref, acc_ref, s_ref, *, t):
    qi = pl.program_id(1)
    key = lax.broadcasted_iota(jnp.int32, (t, t), 0)
    qry = lax.broadcasted_iota(jnp.int32, (t, t), 1)
    causal = key <= qry
    lane = lax.broadcasted_iota(jnp.int32, (t, LANES), 1)
    lower = lane < FOX_DIM

    m_ref[...] = jnp.full(m_ref.shape, -jnp.inf, F32)
    l_ref[...] = jnp.zeros(l_ref.shape, F32)
    acc_ref[...] = jnp.zeros(acc_ref.shape, F32)

    def q_aug(h):
        lanes = slice((h // 2) * LANES, (h // 2 + 1) * LANES)
        q_blk = q_ref[0, :, lanes]
        q_h = jnp.where(lower if h % 2 == 0 else jnp.logical_not(lower), q_blk, jnp.zeros_like(q_blk))
        pick = (lane % FOX_HEADS == h) & (lane < 3 * FOX_HEADS)
        return jnp.concatenate([q_h, jnp.where(pick, 1.0, 0.0).astype(BF16)], axis=1)

    def kv_step(kv, masked):
        off = pl.multiple_of(kv * t, t)
        decay = nc_ref[0, pl.ds(off, t), :]

        def scores_to(slot, h):
            lanes = slice((h // 2) * LANES, (h // 2 + 1) * LANES)
            k_aug = jnp.concatenate([k_ref[0, pl.ds(off, t), lanes], decay], axis=1)
            s_ref[slot] = _dot_nt(k_aug, q_aug(h))

        scores_to(0, 0)
        for h in range(FOX_HEADS):
            if h + 1 < FOX_HEADS:
                scores_to((h + 1) % 2, h + 1)
            s = s_ref[h % 2]
            if masked:
                s = jnp.where(causal, s, -jnp.inf)
            m = m_ref[h:h + 1, :]
            m_new = jnp.maximum(m, jnp.max(s, axis=0, keepdims=True))
            p = jnp.exp(s - m_new)
            corr = jnp.exp(m - m_new)
            m_ref[h:h + 1, :] = m_new
            l_ref[h:h + 1, :] = l_ref[h:h + 1, :] * corr + jnp.sum(p, axis=0, keepdims=True)
            v_t = vt_ref[0, h * FOX_DIM:(h + 1) * FOX_DIM, pl.ds(off, t)]
            acc_ref[h] = acc_ref[h] * corr + _dot(v_t, p.astype(BF16))

    def body(kv, carry):
        kv_step(kv, masked=False)
        return carry

    lax.fori_loop(0, qi, body, 0)
    kv_step(qi, masked=True)

    for j in range(WIDTH // LANES):
        pair = jnp.concatenate([acc_ref[2 * j] / l_ref[2 * j:2 * j + 1, :],
                                acc_ref[2 * j + 1] / l_ref[2 * j + 1:2 * j + 2, :]], axis=0)
        o_ref[0, :, j * LANES:(j + 1) * LANES] = pair.T


def _fox_attention(q, k, v_t, decay, t):
    n_seq, seq_len, _ = q.shape
    tile = pl.BlockSpec((1, t, WIDTH), lambda b, i: (b, i, 0))
    return pl.pallas_call(
        functools.partial(_fox_kernel, t=t),
        grid=(n_seq, seq_len // t),
        in_specs=[tile,
                  pl.BlockSpec((1, seq_len, WIDTH), lambda b, i: (b, 0, 0)),
                  pl.BlockSpec((1, WIDTH, seq_len), lambda b, i: (b, 0, 0)),
                  pl.BlockSpec((1, seq_len, LANES), lambda b, i: (b, 0, 0))],
        out_specs=tile,
        out_shape=jax.ShapeDtypeStruct((n_seq, seq_len, WIDTH), F32),
        scratch_shapes=[pltpu.VMEM((FOX_HEADS, t), F32), pltpu.VMEM((FOX_HEADS, t), F32),
                        pltpu.VMEM((FOX_HEADS, FOX_DIM, t), F32), pltpu.VMEM((2, t, t), F32)],
        compiler_params=_params(2), name="fox_attention",
    )(q, k, v_t, decay)


def _mem_kv_kernel(mem_ref, g_ref, w_ref, k_ref, v_ref, kb_ref, vb_ref):
    h = _rms(mem_ref[0], g_ref[...]).astype(BF16)
    kv = _dot(h, w_ref[...])
    k_ref[0] = kv[:, :WIDTH]
    v_ref[0] = kv[:, WIDTH:]
    kb_ref[0] = kv[:, :WIDTH].astype(BF16)
    vb_ref[0] = kv[:, WIDTH:].astype(BF16)


def _mem_kv(mem, g_mem, w_mem_kv):
    n_seq = mem.shape[0]
    blk = pl.BlockSpec((1, N_MEM, WIDTH), lambda b: (b, 0, 0))
    return pl.pallas_call(
        _mem_kv_kernel, grid=(n_seq,),
        in_specs=[pl.BlockSpec((1, N_MEM, D_MODEL), lambda b: (b, 0, 0)),
                  _resident((1, D_MODEL)), _resident((D_MODEL, 2 * WIDTH))],
        out_specs=[blk] * 4,
        out_shape=[jax.ShapeDtypeStruct((n_seq, N_MEM, WIDTH), F32)] * 2
        + [jax.ShapeDtypeStruct((n_seq, N_MEM, WIDTH), BF16)] * 2,
        compiler_params=_params(1), name="mem_kv",
    )(mem, g_mem, w_mem_kv)


def _gated_merge(branch_ref, wbr_ref, wout_ref, gate_fn):
    merged = None
    for n in range(3):
        proj = _dot(branch_ref[n], wbr_ref[n])
        term = _sigmoid(gate_fn(n)) * proj
        merged = term if merged is None else merged + term
    return _dot(merged.astype(BF16), wout_ref[...])


def _finish_kernel(x_ref, fox_ref, gpre_ref, w_ref, lng_ref, lnb_ref, wsp_ref, bsp_ref,
                   mk_ref, mv_ref, wbr_ref, wout_ref, gpost_ref, y_ref, br_ref, *, tm):
    x = x_ref[...]
    h = _rms(x, gpre_ref[...]).astype(BF16)

    def proj(c0, width=WIDTH):
        return _dot(h, w_ref[:, c0:c0 + width])

    u = _gelu(proj(R_UA))
    v = _layer_norm(_gelu(proj(R_VA)), lng_ref[...], lnb_ref[...]).astype(BF16)
    gate_a = _silu(proj(R_ZA))
    n_chunk = tm // CHUNK
    r2 = lax.broadcasted_iota(jnp.int32, (2 * CHUNK, CHUNK), 0)
    c2 = lax.broadcasted_iota(jnp.int32, (2 * CHUNK, CHUNK), 1)
    tril = (r2 & (CHUNK - 1)) >= c2
    lower = lax.broadcasted_iota(jnp.int32, (CHUNK, LANES), 1) < (LANES // 2)
    for j in range(WIDTH // LANES):
        lanes = slice(j * LANES, (j + 1) * LANES)
        w_pair = jnp.where(tril, wsp_ref[j], 0.0).astype(BF16)
        rhs = jnp.concatenate([v[c * CHUNK:(c + 1) * CHUNK, lanes] for c in range(n_chunk)], axis=1)
        mix = _dot(w_pair, rhs)
        for c in range(n_chunk):
            blk = mix[:, c * LANES:(c + 1) * LANES]
            mixed = jnp.where(lower, blk[:CHUNK], blk[CHUNK:]) + bsp_ref[:, lanes]
            rows = slice(c * CHUNK, (c + 1) * CHUNK)
            br_ref[0, rows, lanes] = (u[rows, lanes] * mixed * gate_a[rows, lanes]).astype(BF16)

    br_ref[1] = (fox_ref[...] * _silu(proj(R_ZB))).astype(BF16)

    qc = proj(R_QC)
    gate_c = _silu(proj(R_ZC))
    for hh in range(MEM_HEADS):
        lanes = slice(hh * MEM_DIM, (hh + 1) * MEM_DIM)
        s = _dot_nt(qc[:, lanes].astype(BF16), mk_ref[0, :, lanes]) * MEM_SCALE
        p = jnp.exp(s - jnp.max(s, axis=-1, keepdims=True))
        o = _dot(p.astype(BF16), mv_ref[0, :, lanes]) / jnp.sum(p, axis=-1, keepdims=True)
        br_ref[2, :, lanes] = (o * gate_c[:, lanes]).astype(BF16)

    y = _gated_merge(br_ref, wbr_ref, wout_ref, lambda n: proj(R_GATE + n * D_MODEL, D_MODEL))
    y_ref[...] = x + _rms(y, gpost_ref[...])


def _finish(x2d, fox2d, seq_len, g_pre, w_rest, ln_g, ln_b, w_sp, b_sp, mkb, mvb, w_br, w_out, g_post, tm):
    m = x2d.shape[0]
    per_seq = seq_len // tm
    row = lambda i: (i, 0)
    mem = pl.BlockSpec((1, N_MEM, WIDTH), lambda i: (i // per_seq, 0, 0))
    return pl.pallas_call(
        functools.partial(_finish_kernel, tm=tm),
        grid=(m // tm,),
        in_specs=[pl.BlockSpec((tm, D_MODEL), row), pl.BlockSpec((tm, WIDTH), row),
                  _resident((1, D_MODEL)), _resident((D_MODEL, REST_WIDTH)),
                  _resident((1, WIDTH)), _resident((1, WIDTH)),
                  _resident((WIDTH // LANES, 2 * CHUNK, CHUNK)), _resident((CHUNK, WIDTH)),
                  mem, mem,
                  _resident((3, WIDTH, D_MODEL)), _resident((D_MODEL, D_MODEL)), _resident((1, D_MODEL))],
        out_specs=pl.BlockSpec((tm, D_MODEL), row),
        out_shape=jax.ShapeDtypeStruct((m, D_MODEL), F32),
        scratch_shapes=[pltpu.VMEM((3, tm, WIDTH), BF16)],
        compiler_params=_params(1), name="prompt_finish",
    )(x2d, fox2d, g_pre, w_rest, ln_g, ln_b, w_sp, b_sp, mkb, mvb, w_br, w_out, g_post)


def _norm_proj_kernel(x_ref, g_ref, w_ref, z_ref):
    z_ref[...] = _dot(_rms(x_ref[...], g_ref[...]).astype(BF16), w_ref[...])


def _norm_project(x2d, g_pre, w, tn):
    m, n = x2d.shape[0], w.shape[1]
    return pl.pallas_call(
        _norm_proj_kernel, grid=(n // tn,),
        in_specs=[_resident((m, D_MODEL)), _resident((1, D_MODEL)),
                  pl.BlockSpec((D_MODEL, tn), lambda j: (0, j))],
        out_specs=pl.BlockSpec((m, tn), lambda j: (0, j)),
        out_shape=jax.ShapeDtypeStruct((m, n), F32),
        compiler_params=_params(1), name="sample_rest_project",
    )(x2d, g_pre, w)


def _head_rows(row_vec, n_rows, head_dim):
    width = row_vec.shape[-1]
    r = lax.broadcasted_iota(jnp.int32, (n_rows, width), 0)
    c = lax.broadcasted_iota(jnp.int32, (n_rows, width), 1)
    mask = (c // head_dim) == r
    return jnp.where(mask, jnp.broadcast_to(row_vec, (n_rows, width)), 0.0), mask


def _decode_kernel(pt_ref, q_ref, kn_ref, vn_ref, cn_ref, ck_ref, cv_ref, clf_ref, o_ref,
                   ring, lfbuf, sc, sem_ring, sem_lf, *, n_pages, n_seq):
    b = pl.program_id(0)
    n_grp = n_pages // GROUP

    def page_copy(src, seq, p, slot):
        return pltpu.make_async_copy(src.at[pt_ref[seq, p]], ring.at[slot], sem_ring.at[slot])

    def lf_copy(seq, p, half):
        return pltpu.make_async_copy(clf_ref.at[pt_ref[seq, p]], lfbuf.at[half, p], sem_lf.at[half])

    def start_group(src, seq, grp, parity):
        for u in range(GROUP):
            page_copy(src, seq, grp * GROUP + u, parity * GROUP + u).start()

    def wait_group(src, grp, parity):
        for u in range(GROUP):
            page_copy(src, b, grp * GROUP + u, parity * GROUP + u).wait()

    @pl.when(b == 0)
    def _():
        for p in range(n_pages):
            lf_copy(0, p, 0).start()
        for parity in range(2):
            start_group(ck_ref, 0, parity, parity)

    @pl.when(b + 1 < n_seq)
    def _():
        for p in range(n_pages):
            lf_copy(b + 1, p, (b + 1) % 2).start()

    half = b % 2
    for p in range(n_pages):
        lf_copy(b, p, half).wait()

    lf = lfbuf[half].reshape(n_pages * FOX_HEADS, PAGE)
    r = lax.broadcasted_iota(jnp.int32, (PAGE, PAGE), 0)
    c = lax.broadcasted_iota(jnp.int32, (PAGE, PAGE), 1)
    after = (r > c).astype(BF16)
    ones = jnp.ones((PAGE, PAGE), BF16)
    parts = _split3(lf)
    within = sum(_dot(x, after) for x in parts).reshape(n_pages, FOX_HEADS, PAGE)
    total = sum(_dot(x, ones) for x in parts).reshape(n_pages, FOX_HEADS, PAGE)
    run = jnp.broadcast_to(cn_ref[0], (FOX_HEADS, PAGE))
    for p in reversed(range(n_pages)):
        sc[p] = within[p] + run
        run = run + total[p]

    q_f32, head_mask = _head_rows(q_ref[0], FOX_HEADS, FOX_DIM)
    q_rows = q_f32.astype(BF16)

    def k_group(grp, parity, refill):
        wait_group(ck_ref, grp, parity)
        for u in range(GROUP):
            p = grp * GROUP + u
            sc[p] = sc[p] + _dot(q_rows, ring[parity * GROUP + u].astype(BF16))
        refill(grp, parity)

    def k_pair(i, carry):
        for parity in range(2):
            k_group(2 * i + parity, parity, lambda grp, par: start_group(ck_ref, b, grp + 2, par))
        return carry

    lax.fori_loop(0, n_grp // 2 - 1, k_pair, 0)
    for parity in range(2):
        k_group(n_grp - 2 + parity, parity, lambda grp, par: start_group(cv_ref, b, par, par))

    s_all = sc[...]
    s_new = jnp.sum(q_f32 * kn_ref[0], axis=-1, keepdims=True)
    m = jnp.maximum(jnp.max(jnp.max(s_all, axis=0), axis=-1, keepdims=True), s_new)
    p_all = jnp.exp(s_all - m[None])
    p_new = jnp.exp(s_new - m)
    denom = jnp.sum(jnp.sum(p_all, axis=0), axis=-1, keepdims=True) + p_new
    sc[...] = p_all

    def v_group(grp, parity, acc, refill):
        wait_group(cv_ref, grp, parity)
        for u in range(GROUP):
            p = grp * GROUP + u
            acc = acc + _dot_nt(sc[p].astype(BF16), ring[parity * GROUP + u].astype(BF16))
        refill(grp, parity)
        return acc

    def v_pair(i, acc):
        for parity in range(2):
            acc = v_group(2 * i + parity, parity, acc, lambda grp, par: start_group(cv_ref, b, grp + 2, par))
        return acc

    acc = lax.fori_loop(0, n_grp // 2 - 1, v_pair, jnp.zeros((FOX_HEADS, WIDTH), F32))

    def next_seq(grp, par):
        @pl.when(b + 1 < n_seq)
        def _():
            start_group(ck_ref, b + 1, par, par)

    for parity in range(2):
        acc = v_group(n_grp - 2 + parity, parity, acc, next_seq)
    acc = acc + p_new * vn_ref[0]
    o_ref[0] = jnp.sum(jnp.where(head_mask, acc, 0.0) / denom, axis=0, keepdims=True)


def _fox_decode(page_table, q, k_new, v_new, c_new, cache_kt, cache_vt, cache_lft):
    n_seq, n_pages = page_table.shape
    assert n_pages % RING == 0
    per_seq = lambda shape: pl.BlockSpec((1,) + shape, lambda b, pt: (b, 0, 0))
    hbm = pl.BlockSpec(memory_space=pl.ANY)
    grid_spec = pltpu.PrefetchScalarGridSpec(
        num_scalar_prefetch=1, grid=(n_seq,),
        in_specs=[per_seq((1, WIDTH)), per_seq((1, WIDTH)), per_seq((1, WIDTH)), per_seq((FOX_HEADS, 1)),
                  hbm, hbm, hbm],
        out_specs=per_seq((1, WIDTH)),
        scratch_shapes=[pltpu.VMEM((RING, WIDTH, PAGE), F32),
                        pltpu.VMEM((2, n_pages, FOX_HEADS, PAGE), F32),
                        pltpu.VMEM((n_pages, FOX_HEADS, PAGE), F32),
                        pltpu.SemaphoreType.DMA((RING,)),
                        pltpu.SemaphoreType.DMA((2,))])
    return pl.pallas_call(
        functools.partial(_decode_kernel, n_pages=n_pages, n_seq=n_seq),
        grid_spec=grid_spec,
        out_shape=jax.ShapeDtypeStruct((n_seq, 1, WIDTH), F32),
        compiler_params=_params(1), name="fox_decode",
    )(page_table, q, k_new, v_new, c_new, cache_kt, cache_vt, cache_lft)


def _mem_decode_kernel(q_ref, k_ref, v_ref, o_ref):
    q = q_ref[0]
    q_rows = jnp.concatenate([q, jnp.zeros_like(q)], axis=0).astype(BF16)
    n_flat = k_ref.shape[1]
    sub = lax.broadcasted_iota(jnp.int32, (2 * MEM_HEADS, n_flat), 0)
    lane = lax.broadcasted_iota(jnp.int32, (2 * MEM_HEADS, n_flat), 1)
    own = (lane % MEM_HEADS) == sub
    s = _dot_nt(q_rows, k_ref[0].astype(BF16)) * MEM_SCALE
    m = jnp.max(jnp.where(own, s, -1e30), axis=-1, keepdims=True)
    p = jnp.where(own, jnp.exp(s - m), 0.0)
    o = _dot(p.astype(BF16), v_ref[0].astype(BF16))
    o_ref[0] = o[:MEM_HEADS] / jnp.sum(p, axis=-1, keepdims=True)[:MEM_HEADS]


def _mem_decode(q, mem_k, mem_v):
    n_seq, n_flat, _ = mem_k.shape
    tok = pl.BlockSpec((1, MEM_HEADS, MEM_DIM), lambda b: (b, 0, 0))
    mem = pl.BlockSpec((1, n_flat, MEM_DIM), lambda b: (b, 0, 0))
    return pl.pallas_call(
        _mem_decode_kernel, grid=(n_seq,), in_specs=[tok, mem, mem], out_specs=tok,
        out_shape=jax.ShapeDtypeStruct((n_seq, MEM_HEADS, MEM_DIM), F32),
        compiler_params=_params(1), name="mem_decode",
    )(q, mem_k, mem_v)


def _sample_finish_kernel(x_ref, z_ref, fox_ref, mem_ref, lng_ref, lnb_ref, ws0_ref, bs0_ref,
                          wbr_ref, wout_ref, gpost_ref, y_ref, av_ref, br_ref):
    col = lambda c0, width=WIDTH: z_ref[:, c0:c0 + width]
    v = _layer_norm(_gelu(col(R_VA)), lng_ref[...], lnb_ref[...])
    av_ref[...] = v
    mixed = ws0_ref[...] * v + bs0_ref[...]
    br_ref[0] = (_gelu(col(R_UA)) * mixed * _silu(col(R_ZA))).astype(BF16)
    br_ref[1] = (fox_ref[...] * _silu(col(R_ZB))).astype(BF16)
    br_ref[2] = (mem_ref[...] * _silu(col(R_ZC))).astype(BF16)
    y = _gated_merge(br_ref, wbr_ref, wout_ref, lambda n: col(R_GATE + n * D_MODEL, D_MODEL))
    y_ref[...] = x_ref[...] + _rms(y, gpost_ref[...])


def _sample_finish(x2d, z, fox, mem, ln_g, ln_b, ws0, bs0, w_br, w_out, g_post):
    m = x2d.shape[0]
    args = (x2d, z, fox, mem, ln_g, ln_b, ws0, bs0, w_br, w_out, g_post)
    return pl.pallas_call(
        _sample_finish_kernel, grid=(1,),
        in_specs=[_resident(a.shape) for a in args],
        out_specs=[pl.BlockSpec((m, D_MODEL), lambda i: (0, 0)), pl.BlockSpec((m, WIDTH), lambda i: (0, 0))],
        out_shape=[jax.ShapeDtypeStruct((m, D_MODEL), F32), jax.ShapeDtypeStruct((m, WIDTH), F32)],
        scratch_shapes=[pltpu.VMEM((3, m, WIDTH), BF16)],
        compiler_params=_params(1), name="sample_finish",
    )(*args)


def kernel(x_prompt, x_sample, mem_prompt, cache_k, cache_v, cache_logf, cache_mem_k, cache_mem_v,
           page_table, g_pre, g_post, g_mem, w_in, b_forget, ln_v_g, ln_v_b, w_spatial, b_spatial,
           w_mem_kv, w_branch, w_out):
    depth = g_pre.shape[0]
    assert depth == 1
    n_seq, seq_len, _ = x_prompt.shape
    n_dec, dec_len, _ = x_sample.shape
    assert dec_len == 1
    n_phys = cache_k.shape[1]

    w = w_in[0]
    o_f = 7 * WIDTH
    w_qk = w[:, 3 * WIDTH:5 * WIDTH].astype(BF16)
    w_kvt = w[:, 4 * WIDTH:6 * WIDTH].T.astype(BF16)
    w_ft = jnp.zeros((16, D_MODEL), F32).at[:FOX_HEADS].set(w[:, o_f:o_f + FOX_HEADS].T).astype(BF16)
    w_rest = jnp.concatenate([w[:, :3 * WIDTH], w[:, 6 * WIDTH:o_f], w[:, o_f + FOX_HEADS:]], axis=1).astype(BF16)
    b_f = b_forget[0].reshape(FOX_HEADS, 1)
    gpre, gpost, gmem = g_pre[0][None], g_post[0][None], g_mem[0][None]
    ln_g, ln_b = ln_v_g[0][None], ln_v_b[0][None]
    w_sp = w_spatial[0].reshape(WIDTH // LANES, 2 * CHUNK, CHUNK)
    b_sp = jnp.repeat(b_spatial[0].T, WIDTH // 8, axis=1)
    ws0 = jnp.repeat(w_spatial[0][:, 0, 0], WIDTH // 8)[None]
    bs0 = b_sp[:1]
    w_br = w_branch[0].astype(BF16)
    w_o = w_out[0].astype(BF16)
    w_mkv = w_mem_kv[0].astype(BF16)

    m = n_seq * seq_len
    xp = x_prompt.reshape(m, D_MODEL)
    tq = min(QKV_TILE, seq_len)
    q, kb, vtb, kt, vt, lft = _qkv_project(xp, n_seq, seq_len, gpre, w_qk, w_kvt, w_ft, b_f, tq)
    pieces = _neg_cumsum(lft)
    decay = jnp.pad(pieces.transpose(0, 2, 1), ((0, 0), (0, 0), (0, LANES - 3 * FOX_HEADS))).astype(BF16)
    t_attn = min(ATTN_TILE, seq_len)
    fox = _fox_attention(q.reshape(n_seq, seq_len, WIDTH), kb.reshape(n_seq, seq_len, WIDTH), vtb, decay, t_attn)
    mk, mv, mkb, mvb = _mem_kv(mem_prompt, gmem, w_mkv)
    yp = _finish(xp, fox.reshape(m, WIDTH), seq_len, gpre, w_rest, ln_g, ln_b, w_sp, b_sp, mkb, mvb,
                 w_br, w_o, gpost, min(FINISH_TILE, seq_len))

    xs = x_sample.reshape(n_dec, D_MODEL)
    qs, kbs, _, kts, vts, lfts = _qkv_project(xs, 1, n_dec, gpre, w_qk, w_kvt, w_ft, b_f, n_dec)
    logf_s = lfts[0].T
    zs = _norm_project(xs, gpre, w_rest, REST_WIDTH // 4)
    tok = lambda a: a.astype(F32).reshape(n_dec, 1, WIDTH)
    pages_t = lambda c: c[0].transpose(0, 2, 3, 1).reshape(n_phys, WIDTH, PAGE)
    fox_s = _fox_decode(page_table, tok(qs), tok(kbs), tok(vts[0].T), logf_s.reshape(n_dec, FOX_HEADS, 1),
                        pages_t(cache_k), pages_t(cache_v), cache_logf[0].transpose(0, 2, 1))
    mem_rows = lambda c: c.reshape(n_dec, N_MEM * MEM_HEADS, MEM_DIM)
    mem_s = _mem_decode(zs[:, R_QC:R_QC + WIDTH].reshape(n_dec, MEM_HEADS, MEM_DIM),
                        mem_rows(cache_mem_k), mem_rows(cache_mem_v))
    ys, av = _sample_finish(xs, zs, fox_s.reshape(n_dec, WIDTH), mem_s.reshape(n_dec, WIDTH),
                            ln_g, ln_b, ws0, bs0, w_br, w_o, gpost)

    def token_major(t, n, length):
        return t.reshape(1, n, FOX_HEADS, FOX_DIM, length).transpose(0, 1, 4, 2, 3)

    return (yp.reshape(n_seq, seq_len, D_MODEL),
            ys.reshape(n_dec, 1, D_MODEL),
            token_major(kt, n_seq, seq_len), token_major(vt, n_seq, seq_len),
            lft.transpose(0, 2, 1)[None],
            mk.reshape(1, n_seq, N_MEM, MEM_HEADS, MEM_DIM), mv.reshape(1, n_seq, N_MEM, MEM_HEADS, MEM_DIM),
            token_major(kts, 1, n_dec).reshape(1, n_dec, 1, FOX_HEADS, FOX_DIM),
            token_major(vts, 1, n_dec).reshape(1, n_dec, 1, FOX_HEADS, FOX_DIM),
            logf_s.reshape(1, n_dec, 1, FOX_HEADS),
            av.reshape(1, n_dec, 1, WIDTH))
```

```python
import functools

import jax
import jax.numpy as jnp
from jax import lax
from jax.experimental import pallas as pl
from jax.experimental.pallas import tpu as pltpu

F32 = jnp.float32
BF16 = jnp.bfloat16

D_MODEL = 1024
WIDTH = 512
FOX_HEADS = 8
FOX_DIM = 64
MEM_HEADS = 4
MEM_DIM = 128
N_MEM = 256
CHUNK = 128
PAGE = 128
NORM_EPS = 1e-6
FOX_SCALE = FOX_DIM ** -0.5
MEM_SCALE = MEM_DIM ** -0.5
LANES = 128

R_UA, R_VA, R_ZA, R_ZB, R_QC, R_ZC, R_GATE = (i * WIDTH for i in range(7))
REST_WIDTH = R_GATE + 3 * D_MODEL

QKV_TILE = 512
ATTN_TILE = 512
FINISH_TILE = 256
GROUP = 8
MEM_DECODE_TOKENS = 4
VMEM_LIMIT = 56 * 1024 * 1024

_NT = (((1,), (1,)), ((), ()))


def _dot(a, b):
    return jnp.dot(a, b, preferred_element_type=F32)


def _dot_nt(a, b):
    return lax.dot_general(a, b, _NT, preferred_element_type=F32)


def _rms(x, g):
    ms = jnp.mean(x * x, axis=-1, keepdims=True)
    return x * lax.rsqrt(ms + NORM_EPS) * g


def _sigmoid(x):
    return 1.0 / (1.0 + jnp.exp(-x))


def _silu(x):
    return x * _sigmoid(x)


def _gelu(x):
    return jax.nn.gelu(x)


def _log_sigmoid(x):
    return jnp.minimum(x, 0.0) - jnp.log1p(jnp.exp(-jnp.abs(x)))


def _layer_norm(x, g, b):
    mu = jnp.mean(x, axis=-1, keepdims=True)
    xc = x - mu
    var = jnp.mean(xc * xc, axis=-1, keepdims=True)
    return xc * lax.rsqrt(var + NORM_EPS) * g + b


def _split3(x):
    hi = x.astype(BF16)
    r = x - hi.astype(F32)
    mid = r.astype(BF16)
    lo = (r - mid.astype(F32)).astype(BF16)
    return hi, mid, lo


def _params(n_axes):
    return pltpu.CompilerParams(dimension_semantics=("arbitrary",) * n_axes,
                                vmem_limit_bytes=VMEM_LIMIT)


def _resident(shape):
    return pl.BlockSpec(shape, lambda *_: (0,) * len(shape), pipeline_mode=pl.Buffered(1))


def _qkv_kernel(x_ref, g_ref, w_ref, wkvt_ref, wft_ref, bf_ref, q_ref, kb_ref, vtb_ref, kt_ref, vt_ref, lft_ref):
    h = _rms(x_ref[...], g_ref[...]).astype(BF16)
    z = _dot(h, w_ref[...])
    q_ref[...] = (z[:, :WIDTH] * FOX_SCALE).astype(BF16)
    kb_ref[...] = z[:, WIDTH:].astype(BF16)
    zt = _dot_nt(wkvt_ref[...], h)
    kt_ref[0] = zt[:WIDTH]
    vt_ref[0] = zt[WIDTH:]
    vtb_ref[0] = zt[WIDTH:].astype(BF16)
    ft = _dot_nt(wft_ref[...], h)
    lft_ref[0] = _log_sigmoid(ft[:FOX_HEADS] + bf_ref[...])


def _qkv_project(x2d, n_seq, seq_len, g_pre, w_qk, w_kvt, w_ft, b_f, tm):
    m = x2d.shape[0]
    per_seq = seq_len // tm
    row = lambda i: (i, 0)
    chan = lambda rows: pl.BlockSpec((1, rows, tm), lambda i: (i // per_seq, 0, i % per_seq))
    return pl.pallas_call(
        _qkv_kernel,
        grid=(m // tm,),
        in_specs=[pl.BlockSpec((tm, D_MODEL), row),
                  _resident((1, D_MODEL)),
                  _resident((D_MODEL, 2 * WIDTH)),
                  _resident((2 * WIDTH, D_MODEL)),
                  _resident((16, D_MODEL)),
                  _resident((FOX_HEADS, 1))],
        out_specs=[pl.BlockSpec((tm, WIDTH), row)] * 2 + [chan(WIDTH)] * 3 + [chan(FOX_HEADS)],
        out_shape=[jax.ShapeDtypeStruct((m, WIDTH), BF16)] * 2
        + [jax.ShapeDtypeStruct((n_seq, WIDTH, seq_len), BF16)]
        + [jax.ShapeDtypeStruct((n_seq, WIDTH, seq_len), F32)] * 2
        + [jax.ShapeDtypeStruct((n_seq, FOX_HEADS, seq_len), F32)],
        compiler_params=_params(1),
        name="qkv_project",
    )(x2d, g_pre, w_qk, w_kvt, w_ft, b_f)


def _neg_cumsum_kernel(lft_ref, o_ref):
    x = lft_ref[0]
    pos = lax.broadcasted_iota(jnp.int32, x.shape, 1)
    shift = 1
    while shift < x.shape[-1]:
        x = x + jnp.where(pos >= shift, pltpu.roll(x, shift, 1), 0.0)
        shift *= 2
    hi, mid, lo = _split3(-x)
    o_ref[0] = jnp.concatenate([hi.astype(F32), mid.astype(F32), lo.astype(F32)], axis=0)


def _neg_cumsum(lft):
    n_seq, heads, seq_len = lft.shape
    return pl.pallas_call(
        _neg_cumsum_kernel, grid=(n_seq,),
        in_specs=[pl.BlockSpec((1, heads, seq_len), lambda b: (b, 0, 0))],
        out_specs=pl.BlockSpec((1, 3 * heads, seq_len), lambda b: (b, 0, 0)),
        out_shape=jax.ShapeDtypeStruct((n_seq, 3 * heads, seq_len), F32),
        compiler_params=_params(1), name="decay_cumsum",
    )(lft)


def _fox_kernel(q_ref, k_ref, vt_ref, nc_ref, o_ref, m_ref, l_ref, acc_ref, s_ref, *, t):
    qi = pl.program_id(1)
    key = lax.broadcasted_iota(jnp.int32, (t, t), 0)
    qry = lax.broadcasted_iota(jnp.int32, (t, t), 1)
    causal = key <= qry
    lane = lax.broadcasted_iota(jnp.int32, (t, LANES), 1)
    lower = lane < FOX_DIM

    m_ref[...] = jnp.full(m_ref.shape, -jnp.inf, F32)
    l_ref[...] = jnp.zeros(l_ref.shape, F32)
    acc_ref[...] = jnp.zeros(acc_ref.shape, F32)

    def q_aug(h):
        lanes = slice((h // 2) * LANES, (h // 2 + 1) * LANES)
        q_blk = q_ref[0, :, lanes]
        q_h = jnp.where(lower if h % 2 == 0 else jnp.logical_not(lower), q_blk, jnp.zeros_like(q_blk))
        pick = (lane % FOX_HEADS == h) & (lane < 3 * FOX_HEADS)
        return jnp.concatenate([q_h, jnp.where(pick, 1.0, 0.0).astype(BF16)], axis=1)

    def kv_step(kv, masked):
        off = pl.multiple_of(kv * t, t)
        decay = nc_ref[0, pl.ds(off, t), :]

        def scores_to(slot, h):
            lanes = slice((h // 2) * LANES, (h // 2 + 1) * LANES)
            k_aug = jnp.concatenate([k_ref[0, pl.ds(off, t), lanes], decay], axis=1)
            s_ref[slot] = _dot_nt(k_aug, q_aug(h))

        scores_to(0, 0)
        for h in range(FOX_HEADS):
            if h + 1 < FOX_HEADS:
                scores_to((h + 1) % 2, h + 1)
            s = s_ref[h % 2]
            if masked:
                s = jnp.where(causal, s, -jnp.inf)
            m = m_ref[h:h + 1, :]
            m_new = jnp.maximum(m, jnp.max(s, axis=0, keepdims=True))
            p = jnp.exp(s - m_new)
            corr = jnp.exp(m - m_new)
            m_ref[h:h + 1, :] = m_new
            l_ref[h:h + 1, :] = l_ref[h:h + 1, :] * corr + jnp.sum(p, axis=0, keepdims=True)
            v_t = vt_ref[0, h * FOX_DIM:(h + 1) * FOX_DIM, pl.ds(off, t)]
            acc_ref[h] = acc_ref[h] * corr + _dot(v_t, p.astype(BF16))

    def body(kv, carry):
        kv_step(kv, masked=False)
        return carry

    lax.fori_loop(0, qi, body, 0)
    kv_step(qi, masked=True)

    for j in range(WIDTH // LANES):
        pair = jnp.concatenate([acc_ref[2 * j] / l_ref[2 * j:2 * j + 1, :],
                                acc_ref[2 * j + 1] / l_ref[2 * j + 1:2 * j + 2, :]], axis=0)
        o_ref[0, :, j * LANES:(j + 1) * LANES] = pair.T


def _fox_attention(q, k, v_t, decay, t):
    n_seq, seq_len, _ = q.shape
    tile = pl.BlockSpec((1, t, WIDTH), lambda b, i: (b, i, 0))
    return pl.pallas_call(
        functools.partial(_fox_kernel, t=t),
        grid=(n_seq, seq_len // t),
        in_specs=[tile,
                  pl.BlockSpec((1, seq_len, WIDTH), lambda b, i: (b, 0, 0)),
                  pl.BlockSpec((1, WIDTH, seq_len), lambda b, i: (b, 0, 0)),
                  pl.BlockSpec((1, seq_len, LANES), lambda b, i: (b, 0, 0))],
        out_specs=tile,
        out_shape=jax.ShapeDtypeStruct((n_seq, seq_len, WIDTH), F32),
        scratch_shapes=[pltpu.VMEM((FOX_HEADS, t), F32), pltpu.VMEM((FOX_HEADS, t), F32),
                        pltpu.VMEM((FOX_HEADS, FOX_DIM, t), F32), pltpu.VMEM((2, t, t), F32)],
        compiler_params=_params(2), name="fox_attention",
    )(q, k, v_t, decay)


def _mem_kv_kernel(mem_ref, g_ref, w_ref, k_ref, v_ref, kb_ref, vb_ref):
    h = _rms(mem_ref[0], g_ref[...]).astype(BF16)
    kv = _dot(h, w_ref[...])
    k_ref[0] = kv[:, :WIDTH]
    v_ref[0] = kv[:, WIDTH:]
    kb_ref[0] = kv[:, :WIDTH].astype(BF16)
    vb_ref[0] = kv[:, WIDTH:].astype(BF16)


def _mem_kv(mem, g_mem, w_mem_kv):
    n_seq = mem.shape[0]
    blk = pl.BlockSpec((1, N_MEM, WIDTH), lambda b: (b, 0, 0))
    return pl.pallas_call(
        _mem_kv_kernel, grid=(n_seq,),
        in_specs=[pl.BlockSpec((1, N_MEM, D_MODEL), lambda b: (b, 0, 0)),
                  _resident((1, D_MODEL)), _resident((D_MODEL, 2 * WIDTH))],
        out_specs=[blk] * 4,
        out_shape=[jax.ShapeDtypeStruct((n_seq, N_MEM, WIDTH), F32)] * 2
        + [jax.ShapeDtypeStruct((n_seq, N_MEM, WIDTH), BF16)] * 2,
        compiler_params=_params(1), name="mem_kv",
    )(mem, g_mem, w_mem_kv)


def _gated_merge(branch_ref, wbr_ref, wout_ref, gate_fn):
    merged = None
    for n in range(3):
        proj = _dot(branch_ref[n], wbr_ref[n])
        term = _sigmoid(gate_fn(n)) * proj
        merged = term if merged is None else merged + term
    return _dot(merged.astype(BF16), wout_ref[...])


def _finish_kernel(x_ref, fox_ref, gpre_ref, w_ref, lng_ref, lnb_ref, wsp_ref, bsp_ref,
                   mk_ref, mv_ref, wbr_ref, wout_ref, gpost_ref, y_ref, br_ref, *, tm):
    x = x_ref[...]
    h = _rms(x, gpre_ref[...]).astype(BF16)

    def proj(c0, width=WIDTH):
        return _dot(h, w_ref[:, c0:c0 + width])

    u = _gelu(proj(R_UA))
    v = _layer_norm(_gelu(proj(R_VA)), lng_ref[...], lnb_ref[...]).astype(BF16)
    gate_a = _silu(proj(R_ZA))
    n_chunk = tm // CHUNK
    r2 = lax.broadcasted_iota(jnp.int32, (2 * CHUNK, CHUNK), 0)
    c2 = lax.broadcasted_iota(jnp.int32, (2 * CHUNK, CHUNK), 1)
    tril = (r2 & (CHUNK - 1)) >= c2
    lower = lax.broadcasted_iota(jnp.int32, (CHUNK, LANES), 1) < (LANES // 2)
    for j in range(WIDTH // LANES):
        lanes = slice(j * LANES, (j + 1) * LANES)
        w_pair = jnp.where(tril, wsp_ref[j], 0.0).astype(BF16)
        rhs = jnp.concatenate([v[c * CHUNK:(c + 1) * CHUNK, lanes] for c in range(n_chunk)], axis=1)
        mix = _dot(w_pair, rhs)
        for c in range(n_chunk):
            blk = mix[:, c * LANES:(c + 1) * LANES]
            mixed = jnp.where(lower, blk[:CHUNK], blk[CHUNK:]) + bsp_ref[:, lanes]
            rows = slice(c * CHUNK, (c + 1) * CHUNK)
            br_ref[0, rows, lanes] = (u[rows, lanes] * mixed * gate_a[rows, lanes]).astype(BF16)

    br_ref[1] = (fox_ref[...] * _silu(proj(R_ZB))).astype(BF16)

    qc = proj(R_QC)
    gate_c = _silu(proj(R_ZC))
    for hh in range(MEM_HEADS):
        lanes = slice(hh * MEM_DIM, (hh + 1) * MEM_DIM)
        s = _dot_nt(qc[:, lanes].astype(BF16), mk_ref[0, :, lanes]) * MEM_SCALE
        p = jnp.exp(s - jnp.max(s, axis=-1, keepdims=True))
        o = _dot(p.astype(BF16), mv_ref[0, :, lanes]) / jnp.sum(p, axis=-1, keepdims=True)
        br_ref[2, :, lanes] = (o * gate_c[:, lanes]).astype(BF16)

    y = _gated_merge(br_ref, wbr_ref, wout_ref, lambda n: proj(R_GATE + n * D_MODEL, D_MODEL))
    y_ref[...] = x + _rms(y, gpost_ref[...])


def _finish(x2d, fox2d, seq_len, g_pre, w_rest, ln_g, ln_b, w_sp, b_sp, mkb, mvb, w_br, w_out, g_post, tm):
    m = x2d.shape[0]
    per_seq = seq_len // tm
    row = lambda i: (i, 0)
    mem = pl.BlockSpec((1, N_MEM, WIDTH), lambda i: (i // per_seq, 0, 0))
    return pl.pallas_call(
        functools.partial(_finish_kernel, tm=tm),
        grid=(m // tm,),
        in_specs=[pl.BlockSpec((tm, D_MODEL), row), pl.BlockSpec((tm, WIDTH), row),
                  _resident((1, D_MODEL)), _resident((D_MODEL, REST_WIDTH)),
                  _resident((1, WIDTH)), _resident((1, WIDTH)),
                  _resident((WIDTH // LANES, 2 * CHUNK, CHUNK)), _resident((CHUNK, WIDTH)),
                  mem, mem,
                  _resident((3, WIDTH, D_MODEL)), _resident((D_MODEL, D_MODEL)), _resident((1, D_MODEL))],
        out_specs=pl.BlockSpec((tm, D_MODEL), row),
        out_shape=jax.ShapeDtypeStruct((m, D_MODEL), F32),
        scratch_shapes=[pltpu.VMEM((3, tm, WIDTH), BF16)],
        compiler_params=_params(1), name="prompt_finish",
    )(x2d, fox2d, g_pre, w_rest, ln_g, ln_b, w_sp, b_sp, mkb, mvb, w_br, w_out, g_post)


def _norm_proj_kernel(x_ref, g_ref, w_ref, z_ref):
    z_ref[...] = _dot(_rms(x_ref[...], g_ref[...]).astype(BF16), w_ref[...])


def _norm_project(x2d, g_pre, w, tn):
    m, n = x2d.shape[0], w.shape[1]
    return pl.pallas_call(
        _norm_proj_kernel, grid=(n // tn,),
        in_specs=[_resident((m, D_MODEL)), _resident((1, D_MODEL)),
                  pl.BlockSpec((D_MODEL, tn), lambda j: (0, j))],
        out_specs=pl.BlockSpec((m, tn), lambda j: (0, j)),
        out_shape=jax.ShapeDtypeStruct((m, n), F32),
        compiler_params=_params(1), name="sample_rest_project",
    )(x2d, g_pre, w)


def _head_rows(row_vec, n_rows, head_dim):
    width = row_vec.shape[-1]
    r = lax.broadcasted_iota(jnp.int32, (n_rows, width), 0)
    c = lax.broadcasted_iota(jnp.int32, (n_rows, width), 1)
    mask = (c // head_dim) == r
    return jnp.where(mask, jnp.broadcast_to(row_vec, (n_rows, width)), 0.0), mask


def _decode_kernel(pt_ref, q_ref, kn_ref, vn_ref, cn_ref, ck_ref, cv_ref, clf_ref, o_ref,
                   ring, lfbuf, sc, sem_ring, sem_lf, *, n_pages, n_seq):
    b = pl.program_id(0)
    n_grp = n_pages // GROUP

    def page_copy(src, seq, p, slot):
        return pltpu.make_async_copy(src.at[pt_ref[seq, p]], ring.at[slot], sem_ring.at[slot])

    def lf_copy(seq, p, half):
        return pltpu.make_async_copy(clf_ref.at[pt_ref[seq, p]], lfbuf.at[half, p], sem_lf.at[half])

    def start_group(src, seq, grp):
        for u in range(GROUP):
            page_copy(src, seq, grp * GROUP + u, grp * GROUP + u).start()

    def wait_group(src, grp):
        for u in range(GROUP):
            page_copy(src, b, grp * GROUP + u, grp * GROUP + u).wait()

    @pl.when(b == 0)
    def _():
        for p in range(n_pages):
            lf_copy(0, p, 0).start()
        for grp in range(n_grp):
            start_group(ck_ref, 0, grp)

    @pl.when(b + 1 < n_seq)
    def _():
        for p in range(n_pages):
            lf_copy(b + 1, p, (b + 1) % 2).start()

    half = b % 2
    for p in range(n_pages):
        lf_copy(b, p, half).wait()

    lf = lfbuf[half].reshape(n_pages * FOX_HEADS, PAGE)
    r = lax.broadcasted_iota(jnp.int32, (PAGE, PAGE), 0)
    c = lax.broadcasted_iota(jnp.int32, (PAGE, PAGE), 1)
    after = (r > c).astype(BF16)
    ones = jnp.ones((PAGE, PAGE), BF16)
    parts = _split3(lf)
    within = sum(_dot(x, after) for x in parts).reshape(n_pages, FOX_HEADS, PAGE)
    total = sum(_dot(x, ones) for x in parts).reshape(n_pages, FOX_HEADS, PAGE)
    run = jnp.broadcast_to(cn_ref[0], (FOX_HEADS, PAGE))
    for p in reversed(range(n_pages)):
        sc[p] = within[p] + run
        run = run + total[p]

    q_f32, head_mask = _head_rows(q_ref[0], FOX_HEADS, FOX_DIM)
    q_rows = q_f32.astype(BF16)

    def k_group(grp, carry):
        wait_group(ck_ref, grp)
        for u in range(GROUP):
            p = grp * GROUP + u
            sc[p] = sc[p] + _dot(q_rows, ring[p].astype(BF16))
        start_group(cv_ref, b, grp)
        return carry

    lax.fori_loop(0, n_grp, k_group, 0)

    s_all = sc[...]
    s_new = jnp.sum(q_f32 * kn_ref[0], axis=-1, keepdims=True)
    m = jnp.maximum(jnp.max(jnp.max(s_all, axis=0), axis=-1, keepdims=True), s_new)
    p_all = jnp.exp(s_all - m[None])
    p_new = jnp.exp(s_new - m)
    denom = jnp.sum(jnp.sum(p_all, axis=0), axis=-1, keepdims=True) + p_new
    sc[...] = p_all

    def v_group(grp, acc):
        wait_group(cv_ref, grp)
        for u in range(GROUP):
            p = grp * GROUP + u
            acc = acc + _dot_nt(sc[p].astype(BF16), ring[p].astype(BF16))

        @pl.when(b + 1 < n_seq)
        def _():
            start_group(ck_ref, b + 1, grp)

        return acc

    acc = lax.fori_loop(0, n_grp, v_group, jnp.zeros((FOX_HEADS, WIDTH), F32))
    acc = acc + p_new * vn_ref[0]
    o_ref[0] = jnp.sum(jnp.where(head_mask, acc, 0.0) / denom, axis=0, keepdims=True)


def _fox_decode(page_table, q, k_new, v_new, c_new, cache_kt, cache_vt, cache_lft):
    n_seq, n_pages = page_table.shape
    assert n_pages % GROUP == 0
    ring_bytes = n_pages * WIDTH * PAGE * 4
    assert ring_bytes <= VMEM_LIMIT // 2
    per_seq = lambda shape: pl.BlockSpec((1,) + shape, lambda b, pt: (b, 0, 0))
    hbm = pl.BlockSpec(memory_space=pl.ANY)
    grid_spec = pltpu.PrefetchScalarGridSpec(
        num_scalar_prefetch=1, grid=(n_seq,),
        in_specs=[per_seq((1, WIDTH)), per_seq((1, WIDTH)), per_seq((1, WIDTH)), per_seq((FOX_HEADS, 1)),
                  hbm, hbm, hbm],
        out_specs=per_seq((1, WIDTH)),
        scratch_shapes=[pltpu.VMEM((n_pages, WIDTH, PAGE), F32),
                        pltpu.VMEM((2, n_pages, FOX_HEADS, PAGE), F32),
                        pltpu.VMEM((n_pages, FOX_HEADS, PAGE), F32),
                        pltpu.SemaphoreType.DMA((n_pages,)),
                        pltpu.SemaphoreType.DMA((2,))])
    return pl.pallas_call(
        functools.partial(_decode_kernel, n_pages=n_pages, n_seq=n_seq),
        grid_spec=grid_spec,
        out_shape=jax.ShapeDtypeStruct((n_seq, 1, WIDTH), F32),
        compiler_params=_params(1), name="fox_decode",
    )(page_table, q, k_new, v_new, c_new, cache_kt, cache_vt, cache_lft)


def _mem_decode_kernel(q_ref, k_ref, v_ref, o_ref):
    n_tok, n_flat, _ = k_ref.shape
    sub = lax.broadcasted_iota(jnp.int32, (2 * MEM_HEADS, n_flat), 0)
    lane = lax.broadcasted_iota(jnp.int32, (2 * MEM_HEADS, n_flat), 1)
    own = (lane % MEM_HEADS) == sub
    for i in range(n_tok):
        q = q_ref[i]
        q_rows = jnp.concatenate([q, jnp.zeros_like(q)], axis=0).astype(BF16)
        s = _dot_nt(q_rows, k_ref[i].astype(BF16)) * MEM_SCALE
        m = jnp.max(jnp.where(own, s, -1e30), axis=-1, keepdims=True)
        p = jnp.where(own, jnp.exp(s - m), 0.0)
        o = _dot(p.astype(BF16), v_ref[i].astype(BF16))
        o_ref[i] = o[:MEM_HEADS] / jnp.sum(p, axis=-1, keepdims=True)[:MEM_HEADS]


def _mem_decode(q, mem_k, mem_v):
    n_seq, n_flat, _ = mem_k.shape
    n_tok = MEM_DECODE_TOKENS if n_seq % MEM_DECODE_TOKENS == 0 else 1
    tok = pl.BlockSpec((n_tok, MEM_HEADS, MEM_DIM), lambda b: (b, 0, 0))
    mem = pl.BlockSpec((n_tok, n_flat, MEM_DIM), lambda b: (b, 0, 0))
    return pl.pallas_call(
        _mem_decode_kernel, grid=(n_seq // n_tok,), in_specs=[tok, mem, mem], out_specs=tok,
        out_shape=jax.ShapeDtypeStruct((n_seq, MEM_HEADS, MEM_DIM), F32),
        compiler_params=_params(1), name="mem_decode",
    )(q, mem_k, mem_v)


def _sample_finish_kernel(x_ref, z_ref, fox_ref, mem_ref, lng_ref, lnb_ref, ws0_ref, bs0_ref,
                          wbr_ref, wout_ref, gpost_ref, y_ref, av_ref, br_ref):
    col = lambda c0, width=WIDTH: z_ref[:, c0:c0 + width]
    v = _layer_norm(_gelu(col(R_VA)), lng_ref[...], lnb_ref[...])
    av_ref[...] = v
    mixed = ws0_ref[...] * v + bs0_ref[...]
    br_ref[0] = (_gelu(col(R_UA)) * mixed * _silu(col(R_ZA))).astype(BF16)
    br_ref[1] = (fox_ref[...] * _silu(col(R_ZB))).astype(BF16)
    br_ref[2] = (mem_ref[...] * _silu(col(R_ZC))).astype(BF16)
    y = _gated_merge(br_ref, wbr_ref, wout_ref, lambda n: col(R_GATE + n * D_MODEL, D_MODEL))
    y_ref[...] = x_ref[...] + _rms(y, gpost_ref[...])


def _sample_finish(x2d, z, fox, mem, ln_g, ln_b, ws0, bs0, w_br, w_out, g_post):
    m = x2d.shape[0]
    args = (x2d, z, fox, mem, ln_g, ln_b, ws0, bs0, w_br, w_out, g_post)
    return pl.pallas_call(
        _sample_finish_kernel, grid=(1,),
        in_specs=[_resident(a.shape) for a in args],
        out_specs=[pl.BlockSpec((m, D_MODEL), lambda i: (0, 0)), pl.BlockSpec((m, WIDTH), lambda i: (0, 0))],
        out_shape=[jax.ShapeDtypeStruct((m, D_MODEL), F32), jax.ShapeDtypeStruct((m, WIDTH), F32)],
        scratch_shapes=[pltpu.VMEM((3, m, WIDTH), BF16)],
        compiler_params=_params(1), name="sample_finish",
    )(*args)


def kernel(x_prompt, x_sample, mem_prompt, cache_k, cache_v, cache_logf, cache_mem_k, cache_mem_v,
           page_table, g_pre, g_post, g_mem, w_in, b_forget, ln_v_g, ln_v_b, w_spatial, b_spatial,
           w_mem_kv, w_branch, w_out):
    depth = g_pre.shape[0]
    assert depth == 1
    n_seq, seq_len, _ = x_prompt.shape
    n_dec, dec_len, _ = x_sample.shape
    assert dec_len == 1
    n_phys = cache_k.shape[1]

    w = w_in[0]
    o_f = 7 * WIDTH
    w_qk = w[:, 3 * WIDTH:5 * WIDTH].astype(BF16)
    w_kvt = w[:, 4 * WIDTH:6 * WIDTH].T.astype(BF16)
    w_ft = jnp.zeros((16, D_MODEL), F32).at[:FOX_HEADS].set(w[:, o_f:o_f + FOX_HEADS].T).astype(BF16)
    w_rest = jnp.concatenate([w[:, :3 * WIDTH], w[:, 6 * WIDTH:o_f], w[:, o_f + FOX_HEADS:]], axis=1).astype(BF16)
    b_f = b_forget[0].reshape(FOX_HEADS, 1)
    gpre, gpost, gmem = g_pre[0][None], g_post[0][None], g_mem[0][None]
    ln_g, ln_b = ln_v_g[0][None], ln_v_b[0][None]
    w_sp = w_spatial[0].reshape(WIDTH // LANES, 2 * CHUNK, CHUNK)
    b_sp = jnp.repeat(b_spatial[0].T, WIDTH // 8, axis=1)
    ws0 = jnp.repeat(w_spatial[0][:, 0, 0], WIDTH // 8)[None]
    bs0 = b_sp[:1]
    w_br = w_branch[0].astype(BF16)
    w_o = w_out[0].astype(BF16)
    w_mkv = w_mem_kv[0].astype(BF16)

    m = n_seq * seq_len
    xp = x_prompt.reshape(m, D_MODEL)
    tq = min(QKV_TILE, seq_len)
    q, kb, vtb, kt, vt, lft = _qkv_project(xp, n_seq, seq_len, gpre, w_qk, w_kvt, w_ft, b_f, tq)
    pieces = _neg_cumsum(lft)
    decay = jnp.pad(pieces.transpose(0, 2, 1), ((0, 0), (0, 0), (0, LANES - 3 * FOX_HEADS))).astype(BF16)
    t_attn = min(ATTN_TILE, seq_len)
    fox = _fox_attention(q.reshape(n_seq, seq_len, WIDTH), kb.reshape(n_seq, seq_len, WIDTH), vtb, decay, t_attn)
    mk, mv, mkb, mvb = _mem_kv(mem_prompt, gmem, w_mkv)
    yp = _finish(xp, fox.reshape(m, WIDTH), seq_len, gpre, w_rest, ln_g, ln_b, w_sp, b_sp, mkb, mvb,
                 w_br, w_o, gpost, min(FINISH_TILE, seq_len))

    xs = x_sample.reshape(n_dec, D_MODEL)
    qs, kbs, _, kts, vts, lfts = _qkv_project(xs, 1, n_dec, gpre, w_qk, w_kvt, w_ft, b_f, n_dec)
    logf_s = lfts[0].T
    zs = _norm_project(xs, gpre, w_rest, REST_WIDTH // 4)
    tok = lambda a: a.astype(F32).reshape(n_dec, 1, WIDTH)
    pages_t = lambda c: c[0].transpose(0, 2, 3, 1).reshape(n_phys, WIDTH, PAGE)
    fox_s = _fox_decode(page_table, tok(qs), tok(kbs), tok(vts[0].T), logf_s.reshape(n_dec, FOX_HEADS, 1),
                        pages_t(cache_k), pages_t(cache_v), cache_logf[0].transpose(0, 2, 1))
    mem_rows = lambda c: c.reshape(n_dec, N_MEM * MEM_HEADS, MEM_DIM)
    mem_s = _mem_decode(zs[:, R_QC:R_QC + WIDTH].reshape(n_dec, MEM_HEADS, MEM_DIM),
                        mem_rows(cache_mem_k), mem_rows(cache_mem_v))
    ys, av = _sample_finish(xs, zs, fox_s.reshape(n_dec, WIDTH), mem_s.reshape(n_dec, WIDTH),
                            ln_g, ln_b, ws0, bs0, w_br, w_o, gpost)

    def token_major(t, n, length):
        return t.reshape(1, n, FOX_HEADS, FOX_DIM, length).transpose(0, 1, 4, 2, 3)

    return (yp.reshape(n_seq, seq_len, D_MODEL),
            ys.reshape(n_dec, 1, D_MODEL),
            token_major(kt, n_seq, seq_len), token_major(vt, n_seq, seq_len),
            lft.transpose(0, 2, 1)[None],
            mk.reshape(1, n_seq, N_MEM, MEM_HEADS, MEM_DIM), mv.reshape(1, n_seq, N_MEM, MEM_HEADS, MEM_DIM),
            token_major(kts, 1, n_dec).reshape(1, n_dec, 1, FOX_HEADS, FOX_DIM),
            token_major(vts, 1, n_dec).reshape(1, n_dec, 1, FOX_HEADS, FOX_DIM),
            logf_s.reshape(1, n_dec, 1, FOX_HEADS),
            av.reshape(1, n_dec, 1, WIDTH))
```

```python
import functools

import jax
import jax.numpy as jnp
from jax import lax
from jax.experimental import pallas as pl
from jax.experimental.pallas import tpu as pltpu

F32 = jnp.float32
BF16 = jnp.bfloat16

D_MODEL = 1024
WIDTH = 512
FOX_HEADS = 8
FOX_DIM = 64
MEM_HEADS = 4
MEM_DIM = 128
N_MEM = 256
CHUNK = 128
PAGE = 128
NORM_EPS = 1e-6
FOX_SCALE = FOX_DIM ** -0.5
MEM_SCALE = MEM_DIM ** -0.5
LANES = 128

R_UA, R_VA, R_ZA, R_ZB, R_QC, R_ZC, R_GATE = (i * WIDTH for i in range(7))
REST_WIDTH = R_GATE + 3 * D_MODEL

QKV_TILE = 512
ATTN_TILE = 512
GROUP = 8
MEM_DECODE_TOKENS = 4
VMEM_LIMIT = 56 * 1024 * 1024

_NT = (((1,), (1,)), ((), ()))


def _dot(a, b):
    return jnp.dot(a, b, preferred_element_type=F32)


def _dot_nt(a, b):
    return lax.dot_general(a, b, _NT, preferred_element_type=F32)


def _rms(x, g):
    ms = jnp.mean(x * x, axis=-1, keepdims=True)
    return x * lax.rsqrt(ms + NORM_EPS) * g


def _sigmoid(x):
    return 1.0 / (1.0 + jnp.exp(-x))


def _silu(x):
    return x * _sigmoid(x)


def _gelu(x):
    return jax.nn.gelu(x)


def _log_sigmoid(x):
    return jnp.minimum(x, 0.0) - jnp.log1p(jnp.exp(-jnp.abs(x)))


def _layer_norm(x, g, b):
    mu = jnp.mean(x, axis=-1, keepdims=True)
    xc = x - mu
    var = jnp.mean(xc * xc, axis=-1, keepdims=True)
    return xc * lax.rsqrt(var + NORM_EPS) * g + b


def _split3(x):
    hi = x.astype(BF16)
    r = x - hi.astype(F32)
    mid = r.astype(BF16)
    lo = (r - mid.astype(F32)).astype(BF16)
    return hi, mid, lo


def _params(n_axes):
    return pltpu.CompilerParams(dimension_semantics=("arbitrary",) * n_axes,
                                vmem_limit_bytes=VMEM_LIMIT)


def _resident(shape):
    return pl.BlockSpec(shape, lambda *_: (0,) * len(shape), pipeline_mode=pl.Buffered(1))


def _qkv_kernel(x_ref, g_ref, w_ref, wkvt_ref, wft_ref, bf_ref, q_ref, kb_ref, vtb_ref, kt_ref, vt_ref, lft_ref):
    h = _rms(x_ref[...], g_ref[...]).astype(BF16)
    z = _dot(h, w_ref[...])
    q_ref[...] = (z[:, :WIDTH] * FOX_SCALE).astype(BF16)
    kb_ref[...] = z[:, WIDTH:].astype(BF16)
    zt = _dot_nt(wkvt_ref[...], h)
    kt_ref[0] = zt[:WIDTH]
    vt_ref[0] = zt[WIDTH:]
    vtb_ref[0] = zt[WIDTH:].astype(BF16)
    ft = _dot_nt(wft_ref[...], h)
    lft_ref[0] = _log_sigmoid(ft[:FOX_HEADS] + bf_ref[...])


def _qkv_project(x2d, n_seq, seq_len, g_pre, w_qk, w_kvt, w_ft, b_f, tm):
    m = x2d.shape[0]
    per_seq = seq_len // tm
    row = lambda i: (i, 0)
    chan = lambda rows: pl.BlockSpec((1, rows, tm), lambda i: (i // per_seq, 0, i % per_seq))
    return pl.pallas_call(
        _qkv_kernel,
        grid=(m // tm,),
        in_specs=[pl.BlockSpec((tm, D_MODEL), row),
                  _resident((1, D_MODEL)),
                  _resident((D_MODEL, 2 * WIDTH)),
                  _resident((2 * WIDTH, D_MODEL)),
                  _resident((16, D_MODEL)),
                  _resident((FOX_HEADS, 1))],
        out_specs=[pl.BlockSpec((tm, WIDTH), row)] * 2 + [chan(WIDTH)] * 3 + [chan(FOX_HEADS)],
        out_shape=[jax.ShapeDtypeStruct((m, WIDTH), BF16)] * 2
        + [jax.ShapeDtypeStruct((n_seq, WIDTH, seq_len), BF16)]
        + [jax.ShapeDtypeStruct((n_seq, WIDTH, seq_len), F32)] * 2
        + [jax.ShapeDtypeStruct((n_seq, FOX_HEADS, seq_len), F32)],
        compiler_params=_params(1),
        name="qkv_project",
    )(x2d, g_pre, w_qk, w_kvt, w_ft, b_f)


def _neg_cumsum_kernel(lft_ref, o_ref):
    x = lft_ref[0]
    pos = lax.broadcasted_iota(jnp.int32, x.shape, 1)
    shift = 1
    while shift < x.shape[-1]:
        x = x + jnp.where(pos >= shift, pltpu.roll(x, shift, 1), 0.0)
        shift *= 2
    hi, mid, lo = _split3(-x)
    o_ref[0] = jnp.concatenate([hi.astype(F32), mid.astype(F32), lo.astype(F32)], axis=0)


def _neg_cumsum(lft):
    n_seq, heads, seq_len = lft.shape
    return pl.pallas_call(
        _neg_cumsum_kernel, grid=(n_seq,),
        in_specs=[pl.BlockSpec((1, heads, seq_len), lambda b: (b, 0, 0))],
        out_specs=pl.BlockSpec((1, 3 * heads, seq_len), lambda b: (b, 0, 0)),
        out_shape=jax.ShapeDtypeStruct((n_seq, 3 * heads, seq_len), F32),
        compiler_params=_params(1), name="decay_cumsum",
    )(lft)


def _fox_kernel(q_ref, k_ref, vt_ref, nc_ref, o_ref, m_ref, l_ref, acc_ref, s_ref, *, t):
    qi = pl.program_id(1)
    key = lax.broadcasted_iota(jnp.int32, (t, t), 0)
    qry = lax.broadcasted_iota(jnp.int32, (t, t), 1)
    causal = key <= qry
    lane = lax.broadcasted_iota(jnp.int32, (t, LANES), 1)
    lower = lane < FOX_DIM

    m_ref[...] = jnp.full(m_ref.shape, -jnp.inf, F32)
    l_ref[...] = jnp.zeros(l_ref.shape, F32)
    acc_ref[...] = jnp.zeros(acc_ref.shape, F32)

    def q_aug(h):
        lanes = slice((h // 2) * LANES, (h // 2 + 1) * LANES)
        q_blk = q_ref[0, :, lanes]
        q_h = jnp.where(lower if h % 2 == 0 else jnp.logical_not(lower), q_blk, jnp.zeros_like(q_blk))
        pick = (lane % FOX_HEADS == h) & (lane < 3 * FOX_HEADS)
        return jnp.concatenate([q_h, jnp.where(pick, 1.0, 0.0).astype(BF16)], axis=1)

    def kv_step(kv, masked):
        off = pl.multiple_of(kv * t, t)
        decay = nc_ref[0, pl.ds(off, t), :]

        def scores_to(slot, h):
            lanes = slice((h // 2) * LANES, (h // 2 + 1) * LANES)
            k_aug = jnp.concatenate([k_ref[0, pl.ds(off, t), lanes], decay], axis=1)
            s_ref[slot] = _dot_nt(k_aug, q_aug(h))

        scores_to(0, 0)
        for h in range(FOX_HEADS):
            if h + 1 < FOX_HEADS:
                scores_to((h + 1) % 2, h + 1)
            s = s_ref[h % 2]
            if masked:
                s = jnp.where(causal, s, -jnp.inf)
            m = m_ref[h:h + 1, :]
            m_new = jnp.maximum(m, jnp.max(s, axis=0, keepdims=True))
            p = jnp.exp(s - m_new)
            corr = jnp.exp(m - m_new)
            m_ref[h:h + 1, :] = m_new
            l_ref[h:h + 1, :] = l_ref[h:h + 1, :] * corr + jnp.sum(p, axis=0, keepdims=True)
            v_t = vt_ref[0, h * FOX_DIM:(h + 1) * FOX_DIM, pl.ds(off, t)]
            acc_ref[h] = acc_ref[h] * corr + _dot(v_t, p.astype(BF16))

    def body(kv, carry):
        kv_step(kv, masked=False)
        return carry

    lax.fori_loop(0, qi, body, 0)
    kv_step(qi, masked=True)

    for j in range(WIDTH // LANES):
        pair = jnp.concatenate([acc_ref[2 * j] / l_ref[2 * j:2 * j + 1, :],
                                acc_ref[2 * j + 1] / l_ref[2 * j + 1:2 * j + 2, :]], axis=0)
        o_ref[0, :, j * LANES:(j + 1) * LANES] = pair.T


def _fox_attention(q, k, v_t, decay, t):
    n_seq, seq_len, _ = q.shape
    tile = pl.BlockSpec((1, t, WIDTH), lambda b, i: (b, i, 0))
    return pl.pallas_call(
        functools.partial(_fox_kernel, t=t),
        grid=(n_seq, seq_len // t),
        in_specs=[tile,
                  pl.BlockSpec((1, seq_len, WIDTH), lambda b, i: (b, 0, 0)),
                  pl.BlockSpec((1, WIDTH, seq_len), lambda b, i: (b, 0, 0)),
                  pl.BlockSpec((1, seq_len, LANES), lambda b, i: (b, 0, 0))],
        out_specs=tile,
        out_shape=jax.ShapeDtypeStruct((n_seq, seq_len, WIDTH), F32),
        scratch_shapes=[pltpu.VMEM((FOX_HEADS, t), F32), pltpu.VMEM((FOX_HEADS, t), F32),
                        pltpu.VMEM((FOX_HEADS, FOX_DIM, t), F32), pltpu.VMEM((2, t, t), F32)],
        compiler_params=_params(2), name="fox_attention",
    )(q, k, v_t, decay)


def _mem_kv_kernel(mem_ref, g_ref, w_ref, k_ref, v_ref, kb_ref, vb_ref):
    h = _rms(mem_ref[0], g_ref[...]).astype(BF16)
    kv = _dot(h, w_ref[...])
    k_ref[0] = kv[:, :WIDTH]
    v_ref[0] = kv[:, WIDTH:]
    kb_ref[0] = kv[:, :WIDTH].astype(BF16)
    vb_ref[0] = kv[:, WIDTH:].astype(BF16)


def _mem_kv(mem, g_mem, w_mem_kv):
    n_seq = mem.shape[0]
    blk = pl.BlockSpec((1, N_MEM, WIDTH), lambda b: (b, 0, 0))
    return pl.pallas_call(
        _mem_kv_kernel, grid=(n_seq,),
        in_specs=[pl.BlockSpec((1, N_MEM, D_MODEL), lambda b: (b, 0, 0)),
                  _resident((1, D_MODEL)), _resident((D_MODEL, 2 * WIDTH))],
        out_specs=[blk] * 4,
        out_shape=[jax.ShapeDtypeStruct((n_seq, N_MEM, WIDTH), F32)] * 2
        + [jax.ShapeDtypeStruct((n_seq, N_MEM, WIDTH), BF16)] * 2,
        compiler_params=_params(1), name="mem_kv",
    )(mem, g_mem, w_mem_kv)


def _gated_merge(branch_ref, wbr_ref, wout_ref, gate_fn):
    merged = None
    for n in range(3):
        proj = _dot(branch_ref[n], wbr_ref[n])
        term = _sigmoid(gate_fn(n)) * proj
        merged = term if merged is None else merged + term
    return _dot(merged.astype(BF16), wout_ref[...])


def _finish_tile(x_ref, fox_ref, z_ref, lng_ref, lnb_ref, wsp_ref, bsp_ref,
                 mk_ref, mv_ref, wbr_ref, wout_ref, gpost_ref, y_ref, br_ref, *, tm):
    def proj(c0, width=WIDTH):
        first = c0 // WIDTH
        return jnp.concatenate([z_ref[first + i] for i in range(width // WIDTH)], axis=1)

    u = _gelu(proj(R_UA))
    v = _layer_norm(_gelu(proj(R_VA)), lng_ref[...], lnb_ref[...]).astype(BF16)
    gate_a = _silu(proj(R_ZA))
    n_chunk = tm // CHUNK
    r2 = lax.broadcasted_iota(jnp.int32, (2 * CHUNK, CHUNK), 0)
    c2 = lax.broadcasted_iota(jnp.int32, (2 * CHUNK, CHUNK), 1)
    tril = (r2 & (CHUNK - 1)) >= c2
    lower = lax.broadcasted_iota(jnp.int32, (CHUNK, LANES), 1) < (LANES // 2)
    for j in range(WIDTH // LANES):
        lanes = slice(j * LANES, (j + 1) * LANES)
        w_pair = jnp.where(tril, wsp_ref[j], 0.0).astype(BF16)
        rhs = jnp.concatenate([v[c * CHUNK:(c + 1) * CHUNK, lanes] for c in range(n_chunk)], axis=1)
        mix = _dot(w_pair, rhs)
        for c in range(n_chunk):
            blk = mix[:, c * LANES:(c + 1) * LANES]
            mixed = jnp.where(lower, blk[:CHUNK], blk[CHUNK:]) + bsp_ref[:, lanes]
            rows = slice(c * CHUNK, (c + 1) * CHUNK)
            br_ref[0, rows, lanes] = (u[rows, lanes] * mixed * gate_a[rows, lanes]).astype(BF16)

    br_ref[1] = (fox_ref[...] * _silu(proj(R_ZB))).astype(BF16)

    qc = proj(R_QC)
    gate_c = _silu(proj(R_ZC))
    for hh in range(MEM_HEADS):
        lanes = slice(hh * MEM_DIM, (hh + 1) * MEM_DIM)
        s = _dot_nt(qc[:, lanes].astype(BF16), mk_ref[0, :, lanes]) * MEM_SCALE
        p = jnp.exp(s - jnp.max(s, axis=-1, keepdims=True))
        o = _dot(p.astype(BF16), mv_ref[0, :, lanes]) / jnp.sum(p, axis=-1, keepdims=True)
        br_ref[2, :, lanes] = (o * gate_c[:, lanes]).astype(BF16)

    y = _gated_merge(br_ref, wbr_ref, wout_ref, lambda n: proj(R_GATE + n * D_MODEL, D_MODEL))
    y_ref[...] = x_ref[...] + _rms(y, gpost_ref[...])


def _norm_proj_kernel(x_ref, g_ref, w_ref, z_ref):
    z_ref[...] = _dot(_rms(x_ref[...], g_ref[...]).astype(BF16), w_ref[0])


def _norm_project(x2d, g_pre, w_chunks):
    m = x2d.shape[0]
    n_chunks, _, tn = w_chunks.shape
    return pl.pallas_call(
        _norm_proj_kernel, grid=(n_chunks,),
        in_specs=[_resident((m, D_MODEL)), _resident((1, D_MODEL)),
                  pl.BlockSpec((1, D_MODEL, tn), lambda j: (j, 0, 0))],
        out_specs=pl.BlockSpec((m, tn), lambda j: (0, j)),
        out_shape=jax.ShapeDtypeStruct((m, n_chunks * tn), F32),
        compiler_params=_params(1), name="sample_rest_project",
    )(x2d, g_pre, w_chunks)


def _head_rows(row_vec, n_rows, head_dim):
    width = row_vec.shape[-1]
    r = lax.broadcasted_iota(jnp.int32, (n_rows, width), 0)
    c = lax.broadcasted_iota(jnp.int32, (n_rows, width), 1)
    mask = (c // head_dim) == r
    return jnp.where(mask, jnp.broadcast_to(row_vec, (n_rows, width)), 0.0), mask


def _hooked_loop(n, body, carry, hook, n_hook):
    n_hook = max(0, min(n_hook, n))

    def both(g, c):
        c = body(g, c)
        hook(g)
        return c

    carry = lax.fori_loop(0, n_hook, both, carry)
    return lax.fori_loop(n_hook, n, body, carry)


def _decode_sequence(b, pt_ref, q_ref, kn_ref, vn_ref, cn_ref, ck_ref, cv_ref, clf_ref, o_ref,
                     ring, lfbuf, sc, sem_ring, sem_lf, *, n_pages, n_seq, hook, n_hook):
    n_grp = n_pages // GROUP

    def page_copy(src, seq, p, slot):
        return pltpu.make_async_copy(src.at[pt_ref[seq, p]], ring.at[slot], sem_ring.at[slot])

    def lf_copy(seq, p, half):
        return pltpu.make_async_copy(clf_ref.at[pt_ref[seq, p]], lfbuf.at[half, p], sem_lf.at[half])

    def start_group(src, seq, grp):
        for u in range(GROUP):
            page_copy(src, seq, grp * GROUP + u, grp * GROUP + u).start()

    def wait_group(src, grp):
        for u in range(GROUP):
            page_copy(src, b, grp * GROUP + u, grp * GROUP + u).wait()

    @pl.when(b == 0)
    def _():
        for p in range(n_pages):
            lf_copy(0, p, 0).start()
        for grp in range(n_grp):
            start_group(ck_ref, 0, grp)

    @pl.when(b + 1 < n_seq)
    def _():
        for p in range(n_pages):
            lf_copy(b + 1, p, (b + 1) % 2).start()

    half = b % 2
    for p in range(n_pages):
        lf_copy(b, p, half).wait()

    lf = lfbuf[half].reshape(n_pages * FOX_HEADS, PAGE)
    r = lax.broadcasted_iota(jnp.int32, (PAGE, PAGE), 0)
    c = lax.broadcasted_iota(jnp.int32, (PAGE, PAGE), 1)
    after = (r > c).astype(BF16)
    ones = jnp.ones((PAGE, PAGE), BF16)
    parts = _split3(lf)
    within = sum(_dot(x, after) for x in parts).reshape(n_pages, FOX_HEADS, PAGE)
    total = sum(_dot(x, ones) for x in parts).reshape(n_pages, FOX_HEADS, PAGE)
    run = jnp.broadcast_to(cn_ref[0], (FOX_HEADS, PAGE))
    for p in reversed(range(n_pages)):
        sc[p] = within[p] + run
        run = run + total[p]

    q_f32, head_mask = _head_rows(q_ref[0], FOX_HEADS, FOX_DIM)
    q_rows = q_f32.astype(BF16)

    def k_group(grp, carry):
        wait_group(ck_ref, grp)
        for u in range(GROUP):
            p = grp * GROUP + u
            sc[p] = sc[p] + _dot(q_rows, ring[p].astype(BF16))
        start_group(cv_ref, b, grp)
        return carry

    _hooked_loop(n_grp, k_group, 0, hook, n_hook)

    s_all = sc[...]
    s_new = jnp.sum(q_f32 * kn_ref[0], axis=-1, keepdims=True)
    m = jnp.maximum(jnp.max(jnp.max(s_all, axis=0), axis=-1, keepdims=True), s_new)
    p_all = jnp.exp(s_all - m[None])
    p_new = jnp.exp(s_new - m)
    denom = jnp.sum(jnp.sum(p_all, axis=0), axis=-1, keepdims=True) + p_new
    sc[...] = p_all

    def v_group(grp, acc):
        wait_group(cv_ref, grp)
        for u in range(GROUP):
            p = grp * GROUP + u
            acc = acc + _dot_nt(sc[p].astype(BF16), ring[p].astype(BF16))

        @pl.when(b + 1 < n_seq)
        def _():
            start_group(ck_ref, b + 1, grp)

        return acc

    acc = _hooked_loop(n_grp, v_group, jnp.zeros((FOX_HEADS, WIDTH), F32),
                       lambda g: hook(n_grp + g), n_hook - n_grp)
    acc = acc + p_new * vn_ref[0]
    o_ref[0] = jnp.sum(jnp.where(head_mask, acc, 0.0) / denom, axis=0, keepdims=True)
    for i in range(2 * n_grp, n_hook):
        hook(i)


def _finish_decode_kernel(pt_ref, x_ref, fox_ref, gpre_ref, w_ref, lng_ref, lnb_ref, wsp_ref, bsp_ref,
                          mk_ref, mv_ref, wbr_ref, wout_ref, gpost_ref,
                          q_ref, kn_ref, vn_ref, cn_ref, ck_ref, cv_ref, clf_ref,
                          y_ref, o_ref,
                          br_ref, h_ref, z_ref, ring, lfbuf, sc, sem_ring, sem_lf, *, tm, n_pages, n_seq):
    h_ref[...] = _rms(x_ref[...], gpre_ref[...]).astype(BF16)

    def project_chunk(c):
        z_ref[c] = _dot(h_ref[...], w_ref[c])

    _decode_sequence(pl.program_id(0), pt_ref, q_ref, kn_ref, vn_ref, cn_ref, ck_ref, cv_ref, clf_ref, o_ref,
                     ring, lfbuf, sc, sem_ring, sem_lf, n_pages=n_pages, n_seq=n_seq,
                     hook=project_chunk, n_hook=REST_WIDTH // WIDTH)
    _finish_tile(x_ref, fox_ref, z_ref, lng_ref, lnb_ref, wsp_ref, bsp_ref, mk_ref, mv_ref,
                 wbr_ref, wout_ref, gpost_ref, y_ref, br_ref, tm=tm)


def _finish_and_decode(x2d, fox2d, seq_len, g_pre, w_chunks, ln_g, ln_b, w_sp, b_sp, mkb, mvb, w_br, w_out, g_post,
                       page_table, q, k_new, v_new, c_new, cache_kt, cache_vt, cache_lft):
    m = x2d.shape[0]
    n_seq, n_pages = page_table.shape
    assert m % n_seq == 0 and seq_len % (m // n_seq) == 0 and (m // n_seq) % CHUNK == 0
    tm = m // n_seq
    assert n_pages % GROUP == 0
    assert n_pages * WIDTH * PAGE * 4 <= VMEM_LIMIT // 2
    per_seq = seq_len // tm
    n_chunks = REST_WIDTH // WIDTH
    row = lambda i, pt: (i, 0)
    mem = pl.BlockSpec((1, N_MEM, WIDTH), lambda i, pt: (i // per_seq, 0, 0))
    tok = lambda shape: pl.BlockSpec((1,) + shape, lambda b, pt: (b, 0, 0))
    hbm = pl.BlockSpec(memory_space=pl.ANY)
    grid_spec = pltpu.PrefetchScalarGridSpec(
        num_scalar_prefetch=1, grid=(n_seq,),
        in_specs=[pl.BlockSpec((tm, D_MODEL), row), pl.BlockSpec((tm, WIDTH), row),
                  _resident((1, D_MODEL)), _resident((n_chunks, D_MODEL, WIDTH)),
                  _resident((1, WIDTH)), _resident((1, WIDTH)),
                  _resident((WIDTH // LANES, 2 * CHUNK, CHUNK)), _resident((CHUNK, WIDTH)),
                  mem, mem,
                  _resident((3, WIDTH, D_MODEL)), _resident((D_MODEL, D_MODEL)), _resident((1, D_MODEL)),
                  tok((1, WIDTH)), tok((1, WIDTH)), tok((1, WIDTH)), tok((FOX_HEADS, 1)),
                  hbm, hbm, hbm],
        out_specs=[pl.BlockSpec((tm, D_MODEL), row), tok((1, WIDTH))],
        scratch_shapes=[pltpu.VMEM((3, tm, WIDTH), BF16),
                        pltpu.VMEM((tm, D_MODEL), BF16),
                        pltpu.VMEM((n_chunks, tm, WIDTH), F32),
                        pltpu.VMEM((n_pages, WIDTH, PAGE), F32),
                        pltpu.VMEM((2, n_pages, FOX_HEADS, PAGE), F32),
                        pltpu.VMEM((n_pages, FOX_HEADS, PAGE), F32),
                        pltpu.SemaphoreType.DMA((n_pages,)),
                        pltpu.SemaphoreType.DMA((2,))])
    return pl.pallas_call(
        functools.partial(_finish_decode_kernel, tm=tm, n_pages=n_pages, n_seq=n_seq),
        grid_spec=grid_spec,
        out_shape=[jax.ShapeDtypeStruct((m, D_MODEL), F32), jax.ShapeDtypeStruct((n_seq, 1, WIDTH), F32)],
        compiler_params=_params(1), name="finish_and_decode",
    )(page_table, x2d, fox2d, g_pre, w_chunks, ln_g, ln_b, w_sp, b_sp, mkb, mvb, w_br, w_out, g_post,
      q, k_new, v_new, c_new, cache_kt, cache_vt, cache_lft)


def _mem_decode_kernel(q_ref, k_ref, v_ref, o_ref):
    n_tok, n_flat, _ = k_ref.shape
    sub = lax.broadcasted_iota(jnp.int32, (2 * MEM_HEADS, n_flat), 0)
    lane = lax.broadcasted_iota(jnp.int32, (2 * MEM_HEADS, n_flat), 1)
    own = (lane % MEM_HEADS) == sub
    for i in range(n_tok):
        q = q_ref[i]
        q_rows = jnp.concatenate([q, jnp.zeros_like(q)], axis=0).astype(BF16)
        s = _dot_nt(q_rows, k_ref[i].astype(BF16)) * MEM_SCALE
        m = jnp.max(jnp.where(own, s, -1e30), axis=-1, keepdims=True)
        p = jnp.where(own, jnp.exp(s - m), 0.0)
        o = _dot(p.astype(BF16), v_ref[i].astype(BF16))
        o_ref[i] = o[:MEM_HEADS] / jnp.sum(p, axis=-1, keepdims=True)[:MEM_HEADS]


def _mem_decode(q, mem_k, mem_v):
    n_seq, n_flat, _ = mem_k.shape
    n_tok = MEM_DECODE_TOKENS if n_seq % MEM_DECODE_TOKENS == 0 else 1
    tok = pl.BlockSpec((n_tok, MEM_HEADS, MEM_DIM), lambda b: (b, 0, 0))
    mem = pl.BlockSpec((n_tok, n_flat, MEM_DIM), lambda b: (b, 0, 0))
    return pl.pallas_call(
        _mem_decode_kernel, grid=(n_seq // n_tok,), in_specs=[tok, mem, mem], out_specs=tok,
        out_shape=jax.ShapeDtypeStruct((n_seq, MEM_HEADS, MEM_DIM), F32),
        compiler_params=_params(1), name="mem_decode",
    )(q, mem_k, mem_v)


def _sample_finish_kernel(x_ref, z_ref, fox_ref, mem_ref, lng_ref, lnb_ref, ws0_ref, bs0_ref,
                          wbr_ref, wout_ref, gpost_ref, y_ref, av_ref, br_ref):
    col = lambda c0, width=WIDTH: z_ref[:, c0:c0 + width]
    v = _layer_norm(_gelu(col(R_VA)), lng_ref[...], lnb_ref[...])
    av_ref[...] = v
    mixed = ws0_ref[...] * v + bs0_ref[...]
    br_ref[0] = (_gelu(col(R_UA)) * mixed * _silu(col(R_ZA))).astype(BF16)
    br_ref[1] = (fox_ref[...] * _silu(col(R_ZB))).astype(BF16)
    br_ref[2] = (mem_ref[...] * _silu(col(R_ZC))).astype(BF16)
    y = _gated_merge(br_ref, wbr_ref, wout_ref, lambda n: col(R_GATE + n * D_MODEL, D_MODEL))
    y_ref[...] = x_ref[...] + _rms(y, gpost_ref[...])


def _sample_finish(x2d, z, fox, mem, ln_g, ln_b, ws0, bs0, w_br, w_out, g_post):
    m = x2d.shape[0]
    args = (x2d, z, fox, mem, ln_g, ln_b, ws0, bs0, w_br, w_out, g_post)
    return pl.pallas_call(
        _sample_finish_kernel, grid=(1,),
        in_specs=[_resident(a.shape) for a in args],
        out_specs=[pl.BlockSpec((m, D_MODEL), lambda i: (0, 0)), pl.BlockSpec((m, WIDTH), lambda i: (0, 0))],
        out_shape=[jax.ShapeDtypeStruct((m, D_MODEL), F32), jax.ShapeDtypeStruct((m, WIDTH), F32)],
        scratch_shapes=[pltpu.VMEM((3, m, WIDTH), BF16)],
        compiler_params=_params(1), name="sample_finish",
    )(*args)


def kernel(x_prompt, x_sample, mem_prompt, cache_k, cache_v, cache_logf, cache_mem_k, cache_mem_v,
           page_table, g_pre, g_post, g_mem, w_in, b_forget, ln_v_g, ln_v_b, w_spatial, b_spatial,
           w_mem_kv, w_branch, w_out):
    depth = g_pre.shape[0]
    assert depth == 1
    n_seq, seq_len, _ = x_prompt.shape
    n_dec, dec_len, _ = x_sample.shape
    assert dec_len == 1
    n_phys = cache_k.shape[1]

    w = w_in[0]
    o_f = 7 * WIDTH
    w_qk = w[:, 3 * WIDTH:5 * WIDTH].astype(BF16)
    w_kvt = w[:, 4 * WIDTH:6 * WIDTH].T.astype(BF16)
    w_ft = jnp.zeros((16, D_MODEL), F32).at[:FOX_HEADS].set(w[:, o_f:o_f + FOX_HEADS].T).astype(BF16)
    w_rest = jnp.concatenate([w[:, :3 * WIDTH], w[:, 6 * WIDTH:o_f], w[:, o_f + FOX_HEADS:]], axis=1).astype(BF16)
    w_chunks = w_rest.reshape(D_MODEL, REST_WIDTH // WIDTH, WIDTH).transpose(1, 0, 2)
    b_f = b_forget[0].reshape(FOX_HEADS, 1)
    gpre, gpost, gmem = g_pre[0][None], g_post[0][None], g_mem[0][None]
    ln_g, ln_b = ln_v_g[0][None], ln_v_b[0][None]
    w_sp = w_spatial[0].reshape(WIDTH // LANES, 2 * CHUNK, CHUNK)
    b_sp = jnp.repeat(b_spatial[0].T, WIDTH // 8, axis=1)
    ws0 = jnp.repeat(w_spatial[0][:, 0, 0], WIDTH // 8)[None]
    bs0 = b_sp[:1]
    w_br = w_branch[0].astype(BF16)
    w_o = w_out[0].astype(BF16)
    w_mkv = w_mem_kv[0].astype(BF16)

    m = n_seq * seq_len
    xp = x_prompt.reshape(m, D_MODEL)
    tq = min(QKV_TILE, seq_len)
    q, kb, vtb, kt, vt, lft = _qkv_project(xp, n_seq, seq_len, gpre, w_qk, w_kvt, w_ft, b_f, tq)
    pieces = _neg_cumsum(lft)
    decay = jnp.pad(pieces.transpose(0, 2, 1), ((0, 0), (0, 0), (0, LANES - 3 * FOX_HEADS))).astype(BF16)
    t_attn = min(ATTN_TILE, seq_len)
    fox = _fox_attention(q.reshape(n_seq, seq_len, WIDTH), kb.reshape(n_seq, seq_len, WIDTH), vtb, decay, t_attn)
    mk, mv, mkb, mvb = _mem_kv(mem_prompt, gmem, w_mkv)

    xs = x_sample.reshape(n_dec, D_MODEL)
    qs, kbs, _, kts, vts, lfts = _qkv_project(xs, 1, n_dec, gpre, w_qk, w_kvt, w_ft, b_f, n_dec)
    logf_s = lfts[0].T
    zs = _norm_project(xs, gpre, w_chunks)

    tok = lambda a: a.astype(F32).reshape(n_dec, 1, WIDTH)
    pages_t = lambda c: c[0].transpose(0, 2, 3, 1).reshape(n_phys, WIDTH, PAGE)
    yp, fox_s = _finish_and_decode(
        xp, fox.reshape(m, WIDTH), seq_len, gpre, w_chunks, ln_g, ln_b, w_sp, b_sp, mkb, mvb, w_br, w_o, gpost,
        page_table, tok(qs), tok(kbs), tok(vts[0].T), logf_s.reshape(n_dec, FOX_HEADS, 1),
        pages_t(cache_k), pages_t(cache_v), cache_logf[0].transpose(0, 2, 1))
    mem_rows = lambda c: c.reshape(n_dec, N_MEM * MEM_HEADS, MEM_DIM)
    mem_s = _mem_decode(zs[:, R_QC:R_QC + WIDTH].reshape(n_dec, MEM_HEADS, MEM_DIM),
                        mem_rows(cache_mem_k), mem_rows(cache_mem_v))
    ys, av = _sample_finish(xs, zs, fox_s.reshape(n_dec, WIDTH), mem_s.reshape(n_dec, WIDTH),
                            ln_g, ln_b, ws0, bs0, w_br, w_o, gpost)

    def token_major(t, n, length):
        return t.reshape(1, n, FOX_HEADS, FOX_DIM, length).transpose(0, 1, 4, 2, 3)

    return (yp.reshape(n_seq, seq_len, D_MODEL),
            ys.reshape(n_dec, 1, D_MODEL),
            token_major(kt, n_seq, seq_len), token_major(vt, n_seq, seq_len),
            lft.transpose(0, 2, 1)[None],
            mk.reshape(1, n_seq, N_MEM, MEM_HEADS, MEM_DIM), mv.reshape(1, n_seq, N_MEM, MEM_HEADS, MEM_DIM),
            token_major(kts, 1, n_dec).reshape(1, n_dec, 1, FOX_HEADS, FOX_DIM),
            token_major(vts, 1, n_dec).reshape(1, n_dec, 1, FOX_HEADS, FOX_DIM),
            logf_s.reshape(1, n_dec, 1, FOX_HEADS),
            av.reshape(1, n_dec, 1, WIDTH))
```

```python
import functools

import jax
import jax.numpy as jnp
from jax import lax
from jax.experimental import pallas as pl
from jax.experimental.pallas import tpu as pltpu

F32 = jnp.float32
BF16 = jnp.bfloat16

D_MODEL = 1024
WIDTH = 512
FOX_HEADS = 8
FOX_DIM = 64
MEM_HEADS = 4
MEM_DIM = 128
N_MEM = 256
CHUNK = 128
PAGE = 128
NORM_EPS = 1e-6
FOX_SCALE = FOX_DIM ** -0.5
MEM_SCALE = MEM_DIM ** -0.5
LANES = 128

R_UA, R_VA, R_ZA, R_ZB, R_QC, R_ZC, R_GATE = (i * WIDTH for i in range(7))
REST_WIDTH = R_GATE + 3 * D_MODEL

QKV_TILE = 512
ATTN_TILE = 512
GROUP = 16
PROJ_CHUNK = 1024
MEM_DECODE_TOKENS = 4
VMEM_LIMIT = 56 * 1024 * 1024

_NT = (((1,), (1,)), ((), ()))


def _dot(a, b):
    return jnp.dot(a, b, preferred_element_type=F32)


def _dot_nt(a, b):
    return lax.dot_general(a, b, _NT, preferred_element_type=F32)


def _rms(x, g):
    ms = jnp.mean(x * x, axis=-1, keepdims=True)
    return x * lax.rsqrt(ms + NORM_EPS) * g


def _sigmoid(x):
    return 1.0 / (1.0 + jnp.exp(-x))


def _silu(x):
    return x * _sigmoid(x)


def _gelu(x):
    return jax.nn.gelu(x)


def _log_sigmoid(x):
    return jnp.minimum(x, 0.0) - jnp.log1p(jnp.exp(-jnp.abs(x)))


def _layer_norm(x, g, b):
    mu = jnp.mean(x, axis=-1, keepdims=True)
    xc = x - mu
    var = jnp.mean(xc * xc, axis=-1, keepdims=True)
    return xc * lax.rsqrt(var + NORM_EPS) * g + b


def _split3(x):
    hi = x.astype(BF16)
    r = x - hi.astype(F32)
    mid = r.astype(BF16)
    lo = (r - mid.astype(F32)).astype(BF16)
    return hi, mid, lo


def _params(n_axes):
    return pltpu.CompilerParams(dimension_semantics=("arbitrary",) * n_axes,
                                vmem_limit_bytes=VMEM_LIMIT)


def _resident(shape):
    return pl.BlockSpec(shape, lambda *_: (0,) * len(shape), pipeline_mode=pl.Buffered(1))


def _qkv_kernel(x_ref, g_ref, w_ref, wkvt_ref, wft_ref, bf_ref, q_ref, kb_ref, vtb_ref, kt_ref, vt_ref, lft_ref):
    h = _rms(x_ref[...], g_ref[...]).astype(BF16)
    z = _dot(h, w_ref[...])
    q_ref[...] = (z[:, :WIDTH] * FOX_SCALE).astype(BF16)
    kb_ref[...] = z[:, WIDTH:].astype(BF16)
    zt = _dot_nt(wkvt_ref[...], h)
    kt_ref[0] = zt[:WIDTH]
    vt_ref[0] = zt[WIDTH:]
    vtb_ref[0] = zt[WIDTH:].astype(BF16)
    ft = _dot_nt(wft_ref[...], h)
    lft_ref[0] = _log_sigmoid(ft[:FOX_HEADS] + bf_ref[...])


def _qkv_project(x2d, n_seq, seq_len, g_pre, w_qk, w_kvt, w_ft, b_f, tm):
    m = x2d.shape[0]
    per_seq = seq_len // tm
    row = lambda i: (i, 0)
    chan = lambda rows: pl.BlockSpec((1, rows, tm), lambda i: (i // per_seq, 0, i % per_seq))
    return pl.pallas_call(
        _qkv_kernel,
        grid=(m // tm,),
        in_specs=[pl.BlockSpec((tm, D_MODEL), row),
                  _resident((1, D_MODEL)),
                  _resident((D_MODEL, 2 * WIDTH)),
                  _resident((2 * WIDTH, D_MODEL)),
                  _resident((16, D_MODEL)),
                  _resident((FOX_HEADS, 1))],
        out_specs=[pl.BlockSpec((tm, WIDTH), row)] * 2 + [chan(WIDTH)] * 3 + [chan(FOX_HEADS)],
        out_shape=[jax.ShapeDtypeStruct((m, WIDTH), BF16)] * 2
        + [jax.ShapeDtypeStruct((n_seq, WIDTH, seq_len), BF16)]
        + [jax.ShapeDtypeStruct((n_seq, WIDTH, seq_len), F32)] * 2
        + [jax.ShapeDtypeStruct((n_seq, FOX_HEADS, seq_len), F32)],
        compiler_params=_params(1),
        name="qkv_project",
    )(x2d, g_pre, w_qk, w_kvt, w_ft, b_f)


def _neg_cumsum_kernel(lft_ref, o_ref):
    x = lft_ref[0]
    pos = lax.broadcasted_iota(jnp.int32, x.shape, 1)
    shift = 1
    while shift < x.shape[-1]:
        x = x + jnp.where(pos >= shift, pltpu.roll(x, shift, 1), 0.0)
        shift *= 2
    hi, mid, lo = _split3(-x)
    o_ref[0] = jnp.concatenate([hi.astype(F32), mid.astype(F32), lo.astype(F32)], axis=0)


def _neg_cumsum(lft):
    n_seq, heads, seq_len = lft.shape
    return pl.pallas_call(
        _neg_cumsum_kernel, grid=(n_seq,),
        in_specs=[pl.BlockSpec((1, heads, seq_len), lambda b: (b, 0, 0))],
        out_specs=pl.BlockSpec((1, 3 * heads, seq_len), lambda b: (b, 0, 0)),
        out_shape=jax.ShapeDtypeStruct((n_seq, 3 * heads, seq_len), F32),
        compiler_params=_params(1), name="decay_cumsum",
    )(lft)


def _fox_kernel(q_ref, k_ref, vt_ref, nc_ref, o_ref, m_ref, l_ref, acc_ref, s_ref, *, t):
    qi = pl.program_id(1)
    key = lax.broadcasted_iota(jnp.int32, (t, t), 0)
    qry = lax.broadcasted_iota(jnp.int32, (t, t), 1)
    causal = key <= qry
    lane = lax.broadcasted_iota(jnp.int32, (t, LANES), 1)
    lower = lane < FOX_DIM

    m_ref[...] = jnp.full(m_ref.shape, -jnp.inf, F32)
    l_ref[...] = jnp.zeros(l_ref.shape, F32)
    acc_ref[...] = jnp.zeros(acc_ref.shape, F32)

    def q_aug(h):
        lanes = slice((h // 2) * LANES, (h // 2 + 1) * LANES)
        q_blk = q_ref[0, :, lanes]
        q_h = jnp.where(lower if h % 2 == 0 else jnp.logical_not(lower), q_blk, jnp.zeros_like(q_blk))
        pick = (lane % FOX_HEADS == h) & (lane < 3 * FOX_HEADS)
        return jnp.concatenate([q_h, jnp.where(pick, 1.0, 0.0).astype(BF16)], axis=1)

    def kv_step(kv, masked):
        off = pl.multiple_of(kv * t, t)
        decay = nc_ref[0, pl.ds(off, t), :]

        def scores_to(slot, h):
            lanes = slice((h // 2) * LANES, (h // 2 + 1) * LANES)
            k_aug = jnp.concatenate([k_ref[0, pl.ds(off, t), lanes], decay], axis=1)
            s_ref[slot] = _dot_nt(k_aug, q_aug(h))

        scores_to(0, 0)
        for h in range(FOX_HEADS):
            if h + 1 < FOX_HEADS:
                scores_to((h + 1) % 2, h + 1)
            s = s_ref[h % 2]
            if masked:
                s = jnp.where(causal, s, -jnp.inf)
            m = m_ref[h:h + 1, :]
            m_new = jnp.maximum(m, jnp.max(s, axis=0, keepdims=True))
            p = jnp.exp(s - m_new)
            corr = jnp.exp(m - m_new)
            m_ref[h:h + 1, :] = m_new
            l_ref[h:h + 1, :] = l_ref[h:h + 1, :] * corr + jnp.sum(p, axis=0, keepdims=True)
            v_t = vt_ref[0, h * FOX_DIM:(h + 1) * FOX_DIM, pl.ds(off, t)]
            acc_ref[h] = acc_ref[h] * corr + _dot(v_t, p.astype(BF16))

    def body(kv, carry):
        kv_step(kv, masked=False)
        return carry

    lax.fori_loop(0, qi, body, 0)
    kv_step(qi, masked=True)

    for j in range(WIDTH // LANES):
        pair = jnp.concatenate([acc_ref[2 * j] / l_ref[2 * j:2 * j + 1, :],
                                acc_ref[2 * j + 1] / l_ref[2 * j + 1:2 * j + 2, :]], axis=0)
        o_ref[0, :, j * LANES:(j + 1) * LANES] = pair.T


def _fox_attention(q, k, v_t, decay, t):
    n_seq, seq_len, _ = q.shape
    tile = pl.BlockSpec((1, t, WIDTH), lambda b, i: (b, i, 0))
    return pl.pallas_call(
        functools.partial(_fox_kernel, t=t),
        grid=(n_seq, seq_len // t),
        in_specs=[tile,
                  pl.BlockSpec((1, seq_len, WIDTH), lambda b, i: (b, 0, 0)),
                  pl.BlockSpec((1, WIDTH, seq_len), lambda b, i: (b, 0, 0)),
                  pl.BlockSpec((1, seq_len, LANES), lambda b, i: (b, 0, 0))],
        out_specs=tile,
        out_shape=jax.ShapeDtypeStruct((n_seq, seq_len, WIDTH), F32),
        scratch_shapes=[pltpu.VMEM((FOX_HEADS, t), F32), pltpu.VMEM((FOX_HEADS, t), F32),
                        pltpu.VMEM((FOX_HEADS, FOX_DIM, t), F32), pltpu.VMEM((2, t, t), F32)],
        compiler_params=_params(2), name="fox_attention",
    )(q, k, v_t, decay)


def _mem_kv_kernel(mem_ref, g_ref, w_ref, k_ref, v_ref, kb_ref, vb_ref):
    h = _rms(mem_ref[0], g_ref[...]).astype(BF16)
    kv = _dot(h, w_ref[...])
    k_ref[0] = kv[:, :WIDTH]
    v_ref[0] = kv[:, WIDTH:]
    kb_ref[0] = kv[:, :WIDTH].astype(BF16)
    vb_ref[0] = kv[:, WIDTH:].astype(BF16)


def _mem_kv(mem, g_mem, w_mem_kv):
    n_seq = mem.shape[0]
    blk = pl.BlockSpec((1, N_MEM, WIDTH), lambda b: (b, 0, 0))
    return pl.pallas_call(
        _mem_kv_kernel, grid=(n_seq,),
        in_specs=[pl.BlockSpec((1, N_MEM, D_MODEL), lambda b: (b, 0, 0)),
                  _resident((1, D_MODEL)), _resident((D_MODEL, 2 * WIDTH))],
        out_specs=[blk] * 4,
        out_shape=[jax.ShapeDtypeStruct((n_seq, N_MEM, WIDTH), F32)] * 2
        + [jax.ShapeDtypeStruct((n_seq, N_MEM, WIDTH), BF16)] * 2,
        compiler_params=_params(1), name="mem_kv",
    )(mem, g_mem, w_mem_kv)


def _gated_merge(branch_ref, wbr_ref, wout_ref, gate_fn):
    merged = None
    for n in range(3):
        proj = _dot(branch_ref[n], wbr_ref[n])
        term = _sigmoid(gate_fn(n)) * proj
        merged = term if merged is None else merged + term
    return _dot(merged.astype(BF16), wout_ref[...])


def _finish_tile(x_ref, fox_ref, z_ref, lng_ref, lnb_ref, wsp_ref, bsp_ref,
                 mk_ref, mv_ref, wbr_ref, wout_ref, gpost_ref, y_ref, br_ref, *, tm):
    def proj(c0, width=WIDTH):
        lane0 = c0 % PROJ_CHUNK
        return z_ref[c0 // PROJ_CHUNK, :, lane0:lane0 + width]

    u = _gelu(proj(R_UA))
    v = _layer_norm(_gelu(proj(R_VA)), lng_ref[...], lnb_ref[...]).astype(BF16)
    gate_a = _silu(proj(R_ZA))
    n_chunk = tm // CHUNK
    r2 = lax.broadcasted_iota(jnp.int32, (2 * CHUNK, CHUNK), 0)
    c2 = lax.broadcasted_iota(jnp.int32, (2 * CHUNK, CHUNK), 1)
    tril = (r2 & (CHUNK - 1)) >= c2
    lower = lax.broadcasted_iota(jnp.int32, (CHUNK, LANES), 1) < (LANES // 2)
    for j in range(WIDTH // LANES):
        lanes = slice(j * LANES, (j + 1) * LANES)
        w_pair = jnp.where(tril, wsp_ref[j], 0.0).astype(BF16)
        rhs = jnp.concatenate([v[c * CHUNK:(c + 1) * CHUNK, lanes] for c in range(n_chunk)], axis=1)
        mix = _dot(w_pair, rhs)
        for c in range(n_chunk):
            blk = mix[:, c * LANES:(c + 1) * LANES]
            mixed = jnp.where(lower, blk[:CHUNK], blk[CHUNK:]) + bsp_ref[:, lanes]
            rows = slice(c * CHUNK, (c + 1) * CHUNK)
            br_ref[0, rows, lanes] = (u[rows, lanes] * mixed * gate_a[rows, lanes]).astype(BF16)

    br_ref[1] = (fox_ref[...] * _silu(proj(R_ZB))).astype(BF16)

    qc = proj(R_QC)
    gate_c = _silu(proj(R_ZC))
    for hh in range(MEM_HEADS):
        lanes = slice(hh * MEM_DIM, (hh + 1) * MEM_DIM)
        s = _dot_nt(qc[:, lanes].astype(BF16), mk_ref[0, :, lanes]) * MEM_SCALE
        p = jnp.exp(s - jnp.max(s, axis=-1, keepdims=True))
        o = _dot(p.astype(BF16), mv_ref[0, :, lanes]) / jnp.sum(p, axis=-1, keepdims=True)
        br_ref[2, :, lanes] = (o * gate_c[:, lanes]).astype(BF16)

    y = _gated_merge(br_ref, wbr_ref, wout_ref, lambda n: proj(R_GATE + n * D_MODEL, D_MODEL))
    y_ref[...] = x_ref[...] + _rms(y, gpost_ref[...])


def _norm_proj_kernel(x_ref, g_ref, w_ref, z_ref):
    z_ref[...] = _dot(_rms(x_ref[...], g_ref[...]).astype(BF16), w_ref[0])


def _norm_project(x2d, g_pre, w_chunks):
    m = x2d.shape[0]
    n_chunks, _, tn = w_chunks.shape
    return pl.pallas_call(
        _norm_proj_kernel, grid=(n_chunks,),
        in_specs=[_resident((m, D_MODEL)), _resident((1, D_MODEL)),
                  pl.BlockSpec((1, D_MODEL, tn), lambda j: (j, 0, 0))],
        out_specs=pl.BlockSpec((m, tn), lambda j: (0, j)),
        out_shape=jax.ShapeDtypeStruct((m, n_chunks * tn), F32),
        compiler_params=_params(1), name="sample_rest_project",
    )(x2d, g_pre, w_chunks)


def _decode_sequence(b, pt_ref, q_ref, kn_ref, vn_ref, cn_ref, ck_ref, cv_ref, clf_ref, o_ref,
                     ring, lfbuf, sc, qb_ref, acc_ref, sem_ring, sem_lf, *, n_pages, n_seq, while_values_load):
    n_grp = n_pages // GROUP

    def page_copy(src, seq, p, slot):
        return pltpu.make_async_copy(src.at[pt_ref[seq, p]], ring.at[slot], sem_ring.at[slot])

    def lf_copy(seq, p, half):
        return pltpu.make_async_copy(clf_ref.at[pt_ref[seq, p]], lfbuf.at[half, p], sem_lf.at[half])

    def start_group(src, seq, grp):
        for u in range(GROUP):
            page_copy(src, seq, grp * GROUP + u, grp * GROUP + u).start()

    def wait_group(src, grp):
        for u in range(GROUP):
            page_copy(src, b, grp * GROUP + u, grp * GROUP + u).wait()

    @pl.when(b == 0)
    def _():
        for p in range(n_pages):
            lf_copy(0, p, 0).start()
        for grp in range(n_grp):
            start_group(ck_ref, 0, grp)

    @pl.when(b + 1 < n_seq)
    def _():
        for p in range(n_pages):
            lf_copy(b + 1, p, (b + 1) % 2).start()

    half = b % 2
    for p in range(n_pages):
        lf_copy(b, p, half).wait()

    lf = lfbuf[half].reshape(n_pages * FOX_HEADS, PAGE)
    r = lax.broadcasted_iota(jnp.int32, (PAGE, PAGE), 0)
    c = lax.broadcasted_iota(jnp.int32, (PAGE, PAGE), 1)
    after = (r > c).astype(BF16)
    ones = jnp.ones((PAGE, PAGE), BF16)
    parts = _split3(lf)
    within = sum(_dot(x, after) for x in parts).reshape(n_pages, FOX_HEADS, PAGE)
    total = sum(_dot(x, ones) for x in parts).reshape(n_pages, FOX_HEADS, PAGE)
    run = jnp.broadcast_to(cn_ref[0], (FOX_HEADS, PAGE))
    for p in reversed(range(n_pages)):
        sc[p] = within[p] + run
        run = run + total[p]

    def head_rows(h):
        return slice(h * FOX_DIM, (h + 1) * FOX_DIM)

    def head_sums(x):
        return jnp.concatenate([jnp.sum(x[head_rows(h)], axis=0, keepdims=True) for h in range(FOX_HEADS)], axis=0)

    def per_row(col):
        return jnp.concatenate([jnp.broadcast_to(col[h:h + 1], (FOX_DIM, 1)) for h in range(FOX_HEADS)], axis=0)

    q_col = q_ref[0]
    qb_ref[...] = jnp.broadcast_to(q_col, (WIDTH, PAGE))
    acc_ref[...] = jnp.zeros((WIDTH, PAGE), F32)

    def k_group(grp, carry):
        wait_group(ck_ref, grp)
        for h in range(FOX_HEADS):
            q_h = qb_ref[head_rows(h), :]
            for u in range(GROUP):
                p = grp * GROUP + u
                row = jnp.sum(ring[p, head_rows(h), :] * q_h, axis=0, keepdims=True)
                sc[p, h:h + 1, :] = sc[p, h:h + 1, :] + row
        start_group(cv_ref, b, grp)
        return carry

    lax.fori_loop(0, n_grp, k_group, 0)

    s_all = sc[...]
    s_new = head_sums(q_col * kn_ref[0])
    m = jnp.maximum(jnp.max(jnp.max(s_all, axis=0), axis=-1, keepdims=True), s_new)
    p_all = jnp.exp(s_all - m[None])
    p_new = jnp.exp(s_new - m)
    denom = jnp.sum(jnp.sum(p_all, axis=0), axis=-1, keepdims=True) + p_new
    sc[...] = p_all

    def v_group(grp, carry):
        wait_group(cv_ref, grp)
        probs = [sc[grp * GROUP + u] for u in range(GROUP)]
        for h in range(FOX_HEADS):
            a = acc_ref[head_rows(h), :]
            for u in range(GROUP):
                a = a + ring[grp * GROUP + u, head_rows(h), :] * probs[u][h:h + 1, :]
            acc_ref[head_rows(h), :] = a

        @pl.when(b + 1 < n_seq)
        def _():
            start_group(ck_ref, b + 1, grp)

        return carry

    while_values_load()
    lax.fori_loop(0, n_grp, v_group, 0)
    out = jnp.sum(acc_ref[...], axis=1, keepdims=True) + per_row(p_new) * vn_ref[0]
    o_ref[0] = out / per_row(denom)


def _finish_decode_kernel(pt_ref, x_ref, fox_ref, gpre_ref, w_ref, lng_ref, lnb_ref, wsp_ref, bsp_ref,
                          mk_ref, mv_ref, wbr_ref, wout_ref, gpost_ref,
                          q_ref, kn_ref, vn_ref, cn_ref, ck_ref, cv_ref, clf_ref,
                          y_ref, o_ref,
                          br_ref, z_ref, ring, lfbuf, sc, qb_ref, acc_ref, sem_ring, sem_lf,
                          *, tm, n_pages, n_seq):
    def project():
        h = _rms(x_ref[...], gpre_ref[...]).astype(BF16)
        for c in range(REST_WIDTH // PROJ_CHUNK):
            z_ref[c] = _dot(h, w_ref[c])

    _decode_sequence(pl.program_id(0), pt_ref, q_ref, kn_ref, vn_ref, cn_ref, ck_ref, cv_ref, clf_ref, o_ref,
                     ring, lfbuf, sc, qb_ref, acc_ref, sem_ring, sem_lf, n_pages=n_pages, n_seq=n_seq,
                     while_values_load=project)
    _finish_tile(x_ref, fox_ref, z_ref, lng_ref, lnb_ref, wsp_ref, bsp_ref, mk_ref, mv_ref,
                 wbr_ref, wout_ref, gpost_ref, y_ref, br_ref, tm=tm)


def _finish_and_decode(x2d, fox2d, seq_len, g_pre, w_chunks, ln_g, ln_b, w_sp, b_sp, mkb, mvb, w_br, w_out, g_post,
                       page_table, q, k_new, v_new, c_new, cache_kt, cache_vt, cache_lft):
    m = x2d.shape[0]
    n_seq, n_pages = page_table.shape
    assert m % n_seq == 0 and seq_len % (m // n_seq) == 0 and (m // n_seq) % CHUNK == 0
    tm = m // n_seq
    assert n_pages % GROUP == 0
    assert n_pages * WIDTH * PAGE * 4 <= VMEM_LIMIT // 2
    per_seq = seq_len // tm
    n_chunks = REST_WIDTH // PROJ_CHUNK
    row = lambda i, pt: (i, 0)
    mem = pl.BlockSpec((1, N_MEM, WIDTH), lambda i, pt: (i // per_seq, 0, 0))
    tok = lambda shape: pl.BlockSpec((1,) + shape, lambda b, pt: (b, 0, 0))
    hbm = pl.BlockSpec(memory_space=pl.ANY)
    grid_spec = pltpu.PrefetchScalarGridSpec(
        num_scalar_prefetch=1, grid=(n_seq,),
        in_specs=[pl.BlockSpec((tm, D_MODEL), row), pl.BlockSpec((tm, WIDTH), row),
                  _resident((1, D_MODEL)), _resident((n_chunks, D_MODEL, PROJ_CHUNK)),
                  _resident((1, WIDTH)), _resident((1, WIDTH)),
                  _resident((WIDTH // LANES, 2 * CHUNK, CHUNK)), _resident((CHUNK, WIDTH)),
                  mem, mem,
                  _resident((3, WIDTH, D_MODEL)), _resident((D_MODEL, D_MODEL)), _resident((1, D_MODEL)),
                  tok((WIDTH, 1)), tok((WIDTH, 1)), tok((WIDTH, 1)), tok((FOX_HEADS, 1)),
                  hbm, hbm, hbm],
        out_specs=[pl.BlockSpec((tm, D_MODEL), row), tok((WIDTH, 1))],
        scratch_shapes=[pltpu.VMEM((3, tm, WIDTH), BF16),
                        pltpu.VMEM((n_chunks, tm, PROJ_CHUNK), F32),
                        pltpu.VMEM((n_pages, WIDTH, PAGE), F32),
                        pltpu.VMEM((2, n_pages, FOX_HEADS, PAGE), F32),
                        pltpu.VMEM((n_pages, FOX_HEADS, PAGE), F32),
                        pltpu.VMEM((WIDTH, PAGE), F32),
                        pltpu.VMEM((WIDTH, PAGE), F32),
                        pltpu.SemaphoreType.DMA((n_pages,)),
                        pltpu.SemaphoreType.DMA((2,))])
    return pl.pallas_call(
        functools.partial(_finish_decode_kernel, tm=tm, n_pages=n_pages, n_seq=n_seq),
        grid_spec=grid_spec,
        out_shape=[jax.ShapeDtypeStruct((m, D_MODEL), F32), jax.ShapeDtypeStruct((n_seq, WIDTH, 1), F32)],
        compiler_params=_params(1), name="finish_and_decode",
    )(page_table, x2d, fox2d, g_pre, w_chunks, ln_g, ln_b, w_sp, b_sp, mkb, mvb, w_br, w_out, g_post,
      q, k_new, v_new, c_new, cache_kt, cache_vt, cache_lft)


def _mem_decode_kernel(q_ref, k_ref, v_ref, o_ref):
    n_tok, n_flat, _ = k_ref.shape
    sub = lax.broadcasted_iota(jnp.int32, (2 * MEM_HEADS, n_flat), 0)
    lane = lax.broadcasted_iota(jnp.int32, (2 * MEM_HEADS, n_flat), 1)
    own = (lane % MEM_HEADS) == sub
    for i in range(n_tok):
        q = q_ref[i]
        q_rows = jnp.concatenate([q, jnp.zeros_like(q)], axis=0).astype(BF16)
        s = _dot_nt(q_rows, k_ref[i].astype(BF16)) * MEM_SCALE
        m = jnp.max(jnp.where(own, s, -1e30), axis=-1, keepdims=True)
        p = jnp.where(own, jnp.exp(s - m), 0.0)
        o = _dot(p.astype(BF16), v_ref[i].astype(BF16))
        o_ref[i] = o[:MEM_HEADS] / jnp.sum(p, axis=-1, keepdims=True)[:MEM_HEADS]


def _mem_decode(q, mem_k, mem_v):
    n_seq, n_flat, _ = mem_k.shape
    n_tok = MEM_DECODE_TOKENS if n_seq % MEM_DECODE_TOKENS == 0 else 1
    tok = pl.BlockSpec((n_tok, MEM_HEADS, MEM_DIM), lambda b: (b, 0, 0))
    mem = pl.BlockSpec((n_tok, n_flat, MEM_DIM), lambda b: (b, 0, 0))
    return pl.pallas_call(
        _mem_decode_kernel, grid=(n_seq // n_tok,), in_specs=[tok, mem, mem], out_specs=tok,
        out_shape=jax.ShapeDtypeStruct((n_seq, MEM_HEADS, MEM_DIM), F32),
        compiler_params=_params(1), name="mem_decode",
    )(q, mem_k, mem_v)


def _sample_finish_kernel(x_ref, z_ref, fox_ref, mem_ref, lng_ref, lnb_ref, ws0_ref, bs0_ref,
                          wbr_ref, wout_ref, gpost_ref, y_ref, av_ref, br_ref):
    col = lambda c0, width=WIDTH: z_ref[:, c0:c0 + width]
    v = _layer_norm(_gelu(col(R_VA)), lng_ref[...], lnb_ref[...])
    av_ref[...] = v
    mixed = ws0_ref[...] * v + bs0_ref[...]
    br_ref[0] = (_gelu(col(R_UA)) * mixed * _silu(col(R_ZA))).astype(BF16)
    br_ref[1] = (fox_ref[...] * _silu(col(R_ZB))).astype(BF16)
    br_ref[2] = (mem_ref[...] * _silu(col(R_ZC))).astype(BF16)
    y = _gated_merge(br_ref, wbr_ref, wout_ref, lambda n: col(R_GATE + n * D_MODEL, D_MODEL))
    y_ref[...] = x_ref[...] + _rms(y, gpost_ref[...])


def _sample_finish(x2d, z, fox, mem, ln_g, ln_b, ws0, bs0, w_br, w_out, g_post):
    m = x2d.shape[0]
    args = (x2d, z, fox, mem, ln_g, ln_b, ws0, bs0, w_br, w_out, g_post)
    return pl.pallas_call(
        _sample_finish_kernel, grid=(1,),
        in_specs=[_resident(a.shape) for a in args],
        out_specs=[pl.BlockSpec((m, D_MODEL), lambda i: (0, 0)), pl.BlockSpec((m, WIDTH), lambda i: (0, 0))],
        out_shape=[jax.ShapeDtypeStruct((m, D_MODEL), F32), jax.ShapeDtypeStruct((m, WIDTH), F32)],
        scratch_shapes=[pltpu.VMEM((3, m, WIDTH), BF16)],
        compiler_params=_params(1), name="sample_finish",
    )(*args)


def kernel(x_prompt, x_sample, mem_prompt, cache_k, cache_v, cache_logf, cache_mem_k, cache_mem_v,
           page_table, g_pre, g_post, g_mem, w_in, b_forget, ln_v_g, ln_v_b, w_spatial, b_spatial,
           w_mem_kv, w_branch, w_out):
    depth = g_pre.shape[0]
    assert depth == 1
    n_seq, seq_len, _ = x_prompt.shape
    n_dec, dec_len, _ = x_sample.shape
    assert dec_len == 1
    n_phys = cache_k.shape[1]

    w = w_in[0]
    o_f = 7 * WIDTH
    w_qk = w[:, 3 * WIDTH:5 * WIDTH].astype(BF16)
    w_kvt = w[:, 4 * WIDTH:6 * WIDTH].T.astype(BF16)
    w_ft = jnp.zeros((16, D_MODEL), F32).at[:FOX_HEADS].set(w[:, o_f:o_f + FOX_HEADS].T).astype(BF16)
    w_rest = jnp.concatenate([w[:, :3 * WIDTH], w[:, 6 * WIDTH:o_f], w[:, o_f + FOX_HEADS:]], axis=1).astype(BF16)
    w_chunks = w_rest.reshape(D_MODEL, REST_WIDTH // PROJ_CHUNK, PROJ_CHUNK).transpose(1, 0, 2)
    b_f = b_forget[0].reshape(FOX_HEADS, 1)
    gpre, gpost, gmem = g_pre[0][None], g_post[0][None], g_mem[0][None]
    ln_g, ln_b = ln_v_g[0][None], ln_v_b[0][None]
    w_sp = w_spatial[0].reshape(WIDTH // LANES, 2 * CHUNK, CHUNK)
    b_sp = jnp.repeat(b_spatial[0].T, WIDTH // 8, axis=1)
    ws0 = jnp.repeat(w_spatial[0][:, 0, 0], WIDTH // 8)[None]
    bs0 = b_sp[:1]
    w_br = w_branch[0].astype(BF16)
    w_o = w_out[0].astype(BF16)
    w_mkv = w_mem_kv[0].astype(BF16)

    m = n_seq * seq_len
    xp = x_prompt.reshape(m, D_MODEL)
    tq = min(QKV_TILE, seq_len)
    q, kb, vtb, kt, vt, lft = _qkv_project(xp, n_seq, seq_len, gpre, w_qk, w_kvt, w_ft, b_f, tq)
    pieces = _neg_cumsum(lft)
    decay = jnp.pad(pieces.transpose(0, 2, 1), ((0, 0), (0, 0), (0, LANES - 3 * FOX_HEADS))).astype(BF16)
    t_attn = min(ATTN_TILE, seq_len)
    fox = _fox_attention(q.reshape(n_seq, seq_len, WIDTH), kb.reshape(n_seq, seq_len, WIDTH), vtb, decay, t_attn)
    mk, mv, mkb, mvb = _mem_kv(mem_prompt, gmem, w_mkv)

    xs = x_sample.reshape(n_dec, D_MODEL)
    qs, kbs, _, kts, vts, lfts = _qkv_project(xs, 1, n_dec, gpre, w_qk, w_kvt, w_ft, b_f, n_dec)
    logf_s = lfts[0].T
    zs = _norm_project(xs, gpre, w_chunks)

    tok = lambda a: a.astype(F32).reshape(n_dec, WIDTH, 1)
    pages_t = lambda c: c[0].transpose(0, 2, 3, 1).reshape(n_phys, WIDTH, PAGE)
    yp, fox_s = _finish_and_decode(
        xp, fox.reshape(m, WIDTH), seq_len, gpre, w_chunks, ln_g, ln_b, w_sp, b_sp, mkb, mvb, w_br, w_o, gpost,
        page_table, tok(qs), tok(kbs), tok(vts[0].T), logf_s.reshape(n_dec, FOX_HEADS, 1),
        pages_t(cache_k), pages_t(cache_v), cache_logf[0].transpose(0, 2, 1))
    mem_rows = lambda c: c.reshape(n_dec, N_MEM * MEM_HEADS, MEM_DIM)
    mem_s = _mem_decode(zs[:, R_QC:R_QC + WIDTH].reshape(n_dec, MEM_HEADS, MEM_DIM),
                        mem_rows(cache_mem_k), mem_rows(cache_mem_v))
    ys, av = _sample_finish(xs, zs, fox_s.reshape(n_dec, WIDTH), mem_s.reshape(n_dec, WIDTH),
                            ln_g, ln_b, ws0, bs0, w_br, w_o, gpost)

    def token_major(t, n, length):
        return t.reshape(1, n, FOX_HEADS, FOX_DIM, length).transpose(0, 1, 4, 2, 3)

    return (yp.reshape(n_seq, seq_len, D_MODEL),
            ys.reshape(n_dec, 1, D_MODEL),
            token_major(kt, n_seq, seq_len), token_major(vt, n_seq, seq_len),
            lft.transpose(0, 2, 1)[None],
            mk.reshape(1, n_seq, N_MEM, MEM_HEADS, MEM_DIM), mv.reshape(1, n_seq, N_MEM, MEM_HEADS, MEM_DIM),
            token_major(kts, 1, n_dec).reshape(1, n_dec, 1, FOX_HEADS, FOX_DIM),
            token_major(vts, 1, n_dec).reshape(1, n_dec, 1, FOX_HEADS, FOX_DIM),
            logf_s.reshape(1, n_dec, 1, FOX_HEADS),
            av.reshape(1, n_dec, 1, WIDTH))
```

```python
import functools

import jax
import jax.numpy as jnp
from jax import lax
from jax.experimental import pallas as pl
from jax.experimental.pallas import tpu as pltpu

F32 = jnp.float32
BF16 = jnp.bfloat16

D_MODEL = 1024
WIDTH = 512
FOX_HEADS = 8
FOX_DIM = 64
MEM_HEADS = 4
MEM_DIM = 128
N_MEM = 256
CHUNK = 128
PAGE = 128
NORM_EPS = 1e-6
FOX_SCALE = FOX_DIM ** -0.5
MEM_SCALE = MEM_DIM ** -0.5
LANES = 128

R_UA, R_VA, R_ZA, R_ZB, R_QC, R_ZC, R_GATE = (i * WIDTH for i in range(7))
REST_WIDTH = R_GATE + 3 * D_MODEL

QKV_TILE = 512
ATTN_TILE = 512
GROUP = 16
PROJ_CHUNK = 1024
MEM_DECODE_TOKENS = 4
VMEM_LIMIT = 56 * 1024 * 1024

_NT = (((1,), (1,)), ((), ()))


def _dot(a, b):
    return jnp.dot(a, b, preferred_element_type=F32)


def _dot_nt(a, b):
    return lax.dot_general(a, b, _NT, preferred_element_type=F32)


def _rms(x, g):
    ms = jnp.mean(x * x, axis=-1, keepdims=True)
    return x * lax.rsqrt(ms + NORM_EPS) * g


def _sigmoid(x):
    return 1.0 / (1.0 + jnp.exp(-x))


def _silu(x):
    return x * _sigmoid(x)


def _gelu(x):
    return jax.nn.gelu(x)


def _log_sigmoid(x):
    return jnp.minimum(x, 0.0) - jnp.log1p(jnp.exp(-jnp.abs(x)))


def _layer_norm(x, g, b):
    mu = jnp.mean(x, axis=-1, keepdims=True)
    xc = x - mu
    var = jnp.mean(xc * xc, axis=-1, keepdims=True)
    return xc * lax.rsqrt(var + NORM_EPS) * g + b


def _split3(x):
    hi = x.astype(BF16)
    r = x - hi.astype(F32)
    mid = r.astype(BF16)
    lo = (r - mid.astype(F32)).astype(BF16)
    return hi, mid, lo


def _params(n_axes):
    return pltpu.CompilerParams(dimension_semantics=("arbitrary",) * n_axes,
                                vmem_limit_bytes=VMEM_LIMIT)


def _resident(shape):
    return pl.BlockSpec(shape, lambda *_: (0,) * len(shape), pipeline_mode=pl.Buffered(1))


def _qkv_kernel(x_ref, g_ref, w_ref, wkvt_ref, wft_ref, bf_ref, q_ref, kb_ref, vtb_ref, kt_ref, vt_ref, lft_ref):
    h = _rms(x_ref[...], g_ref[...]).astype(BF16)
    z = _dot(h, w_ref[...])
    q_ref[...] = (z[:, :WIDTH] * FOX_SCALE).astype(BF16)
    kb_ref[...] = z[:, WIDTH:].astype(BF16)
    zt = _dot_nt(wkvt_ref[...], h)
    kt_ref[0] = zt[:WIDTH]
    vt_ref[0] = zt[WIDTH:]
    vtb_ref[0] = zt[WIDTH:].astype(BF16)
    ft = _dot_nt(wft_ref[...], h)
    lft_ref[0] = _log_sigmoid(ft[:FOX_HEADS] + bf_ref[...])


def _qkv_project(x2d, n_seq, seq_len, g_pre, w_qk, w_kvt, w_ft, b_f, tm):
    m = x2d.shape[0]
    per_seq = seq_len // tm
    row = lambda i: (i, 0)
    chan = lambda rows: pl.BlockSpec((1, rows, tm), lambda i: (i // per_seq, 0, i % per_seq))
    return pl.pallas_call(
        _qkv_kernel,
        grid=(m // tm,),
        in_specs=[pl.BlockSpec((tm, D_MODEL), row),
                  _resident((1, D_MODEL)),
                  _resident((D_MODEL, 2 * WIDTH)),
                  _resident((2 * WIDTH, D_MODEL)),
                  _resident((16, D_MODEL)),
                  _resident((FOX_HEADS, 1))],
        out_specs=[pl.BlockSpec((tm, WIDTH), row)] * 2 + [chan(WIDTH)] * 3 + [chan(FOX_HEADS)],
        out_shape=[jax.ShapeDtypeStruct((m, WIDTH), BF16)] * 2
        + [jax.ShapeDtypeStruct((n_seq, WIDTH, seq_len), BF16)]
        + [jax.ShapeDtypeStruct((n_seq, WIDTH, seq_len), F32)] * 2
        + [jax.ShapeDtypeStruct((n_seq, FOX_HEADS, seq_len), F32)],
        compiler_params=_params(1),
        name="qkv_project",
    )(x2d, g_pre, w_qk, w_kvt, w_ft, b_f)


def _neg_cumsum_kernel(lft_ref, o_ref):
    x = lft_ref[0]
    pos = lax.broadcasted_iota(jnp.int32, x.shape, 1)
    shift = 1
    while shift < x.shape[-1]:
        x = x + jnp.where(pos >= shift, pltpu.roll(x, shift, 1), 0.0)
        shift *= 2
    hi, mid, lo = _split3(-x)
    o_ref[0] = jnp.concatenate([hi.astype(F32), mid.astype(F32), lo.astype(F32)], axis=0)


def _neg_cumsum(lft):
    n_seq, heads, seq_len = lft.shape
    return pl.pallas_call(
        _neg_cumsum_kernel, grid=(n_seq,),
        in_specs=[pl.BlockSpec((1, heads, seq_len), lambda b: (b, 0, 0))],
        out_specs=pl.BlockSpec((1, 3 * heads, seq_len), lambda b: (b, 0, 0)),
        out_shape=jax.ShapeDtypeStruct((n_seq, 3 * heads, seq_len), F32),
        compiler_params=_params(1), name="decay_cumsum",
    )(lft)


def _fox_kernel(q_ref, k_ref, vt_ref, nc_ref, o_ref, m_ref, l_ref, acc_ref, s_ref, *, t):
    qi = pl.program_id(1)
    key = lax.broadcasted_iota(jnp.int32, (t, t), 0)
    qry = lax.broadcasted_iota(jnp.int32, (t, t), 1)
    causal = key <= qry
    lane = lax.broadcasted_iota(jnp.int32, (t, LANES), 1)
    lower = lane < FOX_DIM

    m_ref[...] = jnp.full(m_ref.shape, -jnp.inf, F32)
    l_ref[...] = jnp.zeros(l_ref.shape, F32)
    acc_ref[...] = jnp.zeros(acc_ref.shape, F32)

    def q_aug(h):
        lanes = slice((h // 2) * LANES, (h // 2 + 1) * LANES)
        q_blk = q_ref[0, :, lanes]
        q_h = jnp.where(lower if h % 2 == 0 else jnp.logical_not(lower), q_blk, jnp.zeros_like(q_blk))
        pick = (lane % FOX_HEADS == h) & (lane < 3 * FOX_HEADS)
        return jnp.concatenate([q_h, jnp.where(pick, 1.0, 0.0).astype(BF16)], axis=1)

    def kv_step(kv, masked):
        off = pl.multiple_of(kv * t, t)
        decay = nc_ref[0, pl.ds(off, t), :]

        def scores_to(slot, h):
            lanes = slice((h // 2) * LANES, (h // 2 + 1) * LANES)
            k_aug = jnp.concatenate([k_ref[0, pl.ds(off, t), lanes], decay], axis=1)
            s_ref[slot] = _dot_nt(k_aug, q_aug(h))

        scores_to(0, 0)
        for h in range(FOX_HEADS):
            if h + 1 < FOX_HEADS:
                scores_to((h + 1) % 2, h + 1)
            s = s_ref[h % 2]
            if masked:
                s = jnp.where(causal, s, -jnp.inf)
            m = m_ref[h:h + 1, :]
            m_new = jnp.maximum(m, jnp.max(s, axis=0, keepdims=True))
            p = jnp.exp(s - m_new)
            corr = jnp.exp(m - m_new)
            m_ref[h:h + 1, :] = m_new
            l_ref[h:h + 1, :] = l_ref[h:h + 1, :] * corr + jnp.sum(p, axis=0, keepdims=True)
            v_t = vt_ref[0, h * FOX_DIM:(h + 1) * FOX_DIM, pl.ds(off, t)]
            acc_ref[h] = acc_ref[h] * corr + _dot(v_t, p.astype(BF16))

    def body(kv, carry):
        kv_step(kv, masked=False)
        return carry

    lax.fori_loop(0, qi, body, 0)
    kv_step(qi, masked=True)

    for j in range(WIDTH // LANES):
        pair = jnp.concatenate([acc_ref[2 * j] / l_ref[2 * j:2 * j + 1, :],
                                acc_ref[2 * j + 1] / l_ref[2 * j + 1:2 * j + 2, :]], axis=0)
        o_ref[0, :, j * LANES:(j + 1) * LANES] = pair.T


def _fox_attention(q, k, v_t, decay, t):
    n_seq, seq_len, _ = q.shape
    tile = pl.BlockSpec((1, t, WIDTH), lambda b, i: (b, i, 0))
    return pl.pallas_call(
        functools.partial(_fox_kernel, t=t),
        grid=(n_seq, seq_len // t),
        in_specs=[tile,
                  pl.BlockSpec((1, seq_len, WIDTH), lambda b, i: (b, 0, 0)),
                  pl.BlockSpec((1, WIDTH, seq_len), lambda b, i: (b, 0, 0)),
                  pl.BlockSpec((1, seq_len, LANES), lambda b, i: (b, 0, 0))],
        out_specs=tile,
        out_shape=jax.ShapeDtypeStruct((n_seq, seq_len, WIDTH), F32),
        scratch_shapes=[pltpu.VMEM((FOX_HEADS, t), F32), pltpu.VMEM((FOX_HEADS, t), F32),
                        pltpu.VMEM((FOX_HEADS, FOX_DIM, t), F32), pltpu.VMEM((2, t, t), F32)],
        compiler_params=_params(2), name="fox_attention",
    )(q, k, v_t, decay)


def _mem_kv_kernel(mem_ref, g_ref, w_ref, k_ref, v_ref, kb_ref, vb_ref):
    h = _rms(mem_ref[0], g_ref[...]).astype(BF16)
    kv = _dot(h, w_ref[...])
    k_ref[0] = kv[:, :WIDTH]
    v_ref[0] = kv[:, WIDTH:]
    kb_ref[0] = kv[:, :WIDTH].astype(BF16)
    vb_ref[0] = kv[:, WIDTH:].astype(BF16)


def _mem_kv(mem, g_mem, w_mem_kv):
    n_seq = mem.shape[0]
    blk = pl.BlockSpec((1, N_MEM, WIDTH), lambda b: (b, 0, 0))
    return pl.pallas_call(
        _mem_kv_kernel, grid=(n_seq,),
        in_specs=[pl.BlockSpec((1, N_MEM, D_MODEL), lambda b: (b, 0, 0)),
                  _resident((1, D_MODEL)), _resident((D_MODEL, 2 * WIDTH))],
        out_specs=[blk] * 4,
        out_shape=[jax.ShapeDtypeStruct((n_seq, N_MEM, WIDTH), F32)] * 2
        + [jax.ShapeDtypeStruct((n_seq, N_MEM, WIDTH), BF16)] * 2,
        compiler_params=_params(1), name="mem_kv",
    )(mem, g_mem, w_mem_kv)


def _gated_merge(branch_ref, wbr_ref, wout_ref, gate_fn):
    merged = None
    for n in range(3):
        proj = _dot(branch_ref[n], wbr_ref[n])
        term = _sigmoid(gate_fn(n)) * proj
        merged = term if merged is None else merged + term
    return _dot(merged.astype(BF16), wout_ref[...])


def _finish_tile(x_ref, fox_ref, z_ref, lng_ref, lnb_ref, wsp_ref, bsp_ref,
                 mk_ref, mv_ref, wbr_ref, wout_ref, gpost_ref, y_ref, br_ref, *, tm):
    def proj(c0, width=WIDTH):
        lane0 = c0 % PROJ_CHUNK
        return z_ref[c0 // PROJ_CHUNK, :, lane0:lane0 + width]

    u = _gelu(proj(R_UA))
    v = _layer_norm(_gelu(proj(R_VA)), lng_ref[...], lnb_ref[...]).astype(BF16)
    gate_a = _silu(proj(R_ZA))
    n_chunk = tm // CHUNK
    r2 = lax.broadcasted_iota(jnp.int32, (2 * CHUNK, CHUNK), 0)
    c2 = lax.broadcasted_iota(jnp.int32, (2 * CHUNK, CHUNK), 1)
    tril = (r2 & (CHUNK - 1)) >= c2
    lower = lax.broadcasted_iota(jnp.int32, (CHUNK, LANES), 1) < (LANES // 2)
    for j in range(WIDTH // LANES):
        lanes = slice(j * LANES, (j + 1) * LANES)
        w_pair = jnp.where(tril, wsp_ref[j], 0.0).astype(BF16)
        rhs = jnp.concatenate([v[c * CHUNK:(c + 1) * CHUNK, lanes] for c in range(n_chunk)], axis=1)
        mix = _dot(w_pair, rhs)
        for c in range(n_chunk):
            blk = mix[:, c * LANES:(c + 1) * LANES]
            mixed = jnp.where(lower, blk[:CHUNK], blk[CHUNK:]) + bsp_ref[:, lanes]
            rows = slice(c * CHUNK, (c + 1) * CHUNK)
            br_ref[0, rows, lanes] = (u[rows, lanes] * mixed * gate_a[rows, lanes]).astype(BF16)

    br_ref[1] = (fox_ref[...] * _silu(proj(R_ZB))).astype(BF16)

    qc = proj(R_QC)
    gate_c = _silu(proj(R_ZC))
    for hh in range(MEM_HEADS):
        lanes = slice(hh * MEM_DIM, (hh + 1) * MEM_DIM)
        s = _dot_nt(qc[:, lanes].astype(BF16), mk_ref[0, :, lanes]) * MEM_SCALE
        p = jnp.exp(s - jnp.max(s, axis=-1, keepdims=True))
        o = _dot(p.astype(BF16), mv_ref[0, :, lanes]) / jnp.sum(p, axis=-1, keepdims=True)
        br_ref[2, :, lanes] = (o * gate_c[:, lanes]).astype(BF16)

    y = _gated_merge(br_ref, wbr_ref, wout_ref, lambda n: proj(R_GATE + n * D_MODEL, D_MODEL))
    y_ref[...] = x_ref[...] + _rms(y, gpost_ref[...])


def _norm_proj_kernel(x_ref, g_ref, w_ref, z_ref):
    z_ref[...] = _dot(_rms(x_ref[...], g_ref[...]).astype(BF16), w_ref[0])


def _norm_project(x2d, g_pre, w_chunks):
    m = x2d.shape[0]
    n_chunks, _, tn = w_chunks.shape
    return pl.pallas_call(
        _norm_proj_kernel, grid=(n_chunks,),
        in_specs=[_resident((m, D_MODEL)), _resident((1, D_MODEL)),
                  pl.BlockSpec((1, D_MODEL, tn), lambda j: (j, 0, 0))],
        out_specs=pl.BlockSpec((m, tn), lambda j: (0, j)),
        out_shape=jax.ShapeDtypeStruct((m, n_chunks * tn), F32),
        compiler_params=_params(1), name="sample_rest_project",
    )(x2d, g_pre, w_chunks)


def _decode_sequence(b, pt_ref, q_ref, kn_ref, vn_ref, cn_ref, ck_ref, cv_ref, clf_ref, o_ref,
                     ring, lfbuf, sc, acc_ref, sem_ring, sem_lf, *, n_pages, n_seq, while_values_load):
    n_grp = n_pages // GROUP

    def page_copy(src, seq, p, slot):
        return pltpu.make_async_copy(src.at[pt_ref[seq, p]], ring.at[slot], sem_ring.at[slot])

    def lf_copy(seq, p, half):
        return pltpu.make_async_copy(clf_ref.at[pt_ref[seq, p]], lfbuf.at[half, p], sem_lf.at[half])

    def start_group(src, seq, grp):
        for u in range(GROUP):
            page_copy(src, seq, grp * GROUP + u, grp * GROUP + u).start()

    def wait_group(src, grp):
        for u in range(GROUP):
            page_copy(src, b, grp * GROUP + u, grp * GROUP + u).wait()

    @pl.when(b == 0)
    def _():
        for p in range(n_pages):
            lf_copy(0, p, 0).start()
        for grp in range(n_grp):
            start_group(ck_ref, 0, grp)

    @pl.when(b + 1 < n_seq)
    def _():
        for p in range(n_pages):
            lf_copy(b + 1, p, (b + 1) % 2).start()

    half = b % 2
    for p in range(n_pages):
        lf_copy(b, p, half).wait()

    lf = lfbuf[half].reshape(n_pages * FOX_HEADS, PAGE)
    r = lax.broadcasted_iota(jnp.int32, (PAGE, PAGE), 0)
    c = lax.broadcasted_iota(jnp.int32, (PAGE, PAGE), 1)
    after = (r > c).astype(BF16)
    ones = jnp.ones((PAGE, PAGE), BF16)
    parts = _split3(lf)
    within = sum(_dot(x, after) for x in parts).reshape(n_pages, FOX_HEADS, PAGE)
    total = sum(_dot(x, ones) for x in parts).reshape(n_pages, FOX_HEADS, PAGE)
    run = jnp.broadcast_to(cn_ref[0], (FOX_HEADS, PAGE))
    for p in reversed(range(n_pages)):
        sc[p] = within[p] + run
        run = run + total[p]

    def head_rows(h):
        return slice(h * FOX_DIM, (h + 1) * FOX_DIM)

    r8 = lax.broadcasted_iota(jnp.int32, (FOX_HEADS, WIDTH), 0)
    c8 = lax.broadcasted_iota(jnp.int32, (FOX_HEADS, WIDTH), 1)
    head_mask = (c8 // FOX_DIM) == r8

    def on_lanes(col):
        return jnp.sum(jnp.where(head_mask, jnp.broadcast_to(col, head_mask.shape), 0.0), axis=0, keepdims=True)

    q_f32 = jnp.where(head_mask, jnp.broadcast_to(q_ref[0], head_mask.shape), 0.0)
    q_rows = q_f32.astype(BF16)
    acc_ref[...] = jnp.zeros((WIDTH, PAGE), F32)

    def k_group(grp, carry):
        wait_group(ck_ref, grp)
        for u in range(GROUP):
            p = grp * GROUP + u
            sc[p] = sc[p] + _dot(q_rows, ring[p].astype(BF16))
        start_group(cv_ref, b, grp)
        return carry

    lax.fori_loop(0, n_grp, k_group, 0)

    s_all = sc[...]
    s_new = jnp.sum(q_f32 * kn_ref[0], axis=-1, keepdims=True)
    m = jnp.maximum(jnp.max(jnp.max(s_all, axis=0), axis=-1, keepdims=True), s_new)
    p_all = jnp.exp(s_all - m[None])
    p_new = jnp.exp(s_new - m)
    denom = jnp.sum(jnp.sum(p_all, axis=0), axis=-1, keepdims=True) + p_new
    sc[...] = p_all

    def v_group(grp, carry):
        wait_group(cv_ref, grp)
        probs = [sc[grp * GROUP + u] for u in range(GROUP)]
        for h in range(FOX_HEADS):
            a = acc_ref[head_rows(h), :]
            for u in range(GROUP):
                a = a + ring[grp * GROUP + u, head_rows(h), :] * probs[u][h:h + 1, :]
            acc_ref[head_rows(h), :] = a

        @pl.when(b + 1 < n_seq)
        def _():
            start_group(ck_ref, b + 1, grp)

        return carry

    while_values_load()
    lax.fori_loop(0, n_grp, v_group, 0)
    out = jnp.sum(acc_ref[...].T, axis=0, keepdims=True) + on_lanes(p_new) * vn_ref[0]
    o_ref[0] = out / on_lanes(denom)


def _finish_decode_kernel(pt_ref, x_ref, fox_ref, gpre_ref, w_ref, lng_ref, lnb_ref, wsp_ref, bsp_ref,
                          mk_ref, mv_ref, wbr_ref, wout_ref, gpost_ref,
                          q_ref, kn_ref, vn_ref, cn_ref, ck_ref, cv_ref, clf_ref,
                          y_ref, o_ref,
                          br_ref, z_ref, ring, lfbuf, sc, acc_ref, sem_ring, sem_lf,
                          *, tm, n_pages, n_seq):
    def project():
        h = _rms(x_ref[...], gpre_ref[...]).astype(BF16)
        for c in range(REST_WIDTH // PROJ_CHUNK):
            z_ref[c] = _dot(h, w_ref[c])

    _decode_sequence(pl.program_id(0), pt_ref, q_ref, kn_ref, vn_ref, cn_ref, ck_ref, cv_ref, clf_ref, o_ref,
                     ring, lfbuf, sc, acc_ref, sem_ring, sem_lf, n_pages=n_pages, n_seq=n_seq,
                     while_values_load=project)
    _finish_tile(x_ref, fox_ref, z_ref, lng_ref, lnb_ref, wsp_ref, bsp_ref, mk_ref, mv_ref,
                 wbr_ref, wout_ref, gpost_ref, y_ref, br_ref, tm=tm)


def _finish_and_decode(x2d, fox2d, seq_len, g_pre, w_chunks, ln_g, ln_b, w_sp, b_sp, mkb, mvb, w_br, w_out, g_post,
                       page_table, q, k_new, v_new, c_new, cache_kt, cache_vt, cache_lft):
    m = x2d.shape[0]
    n_seq, n_pages = page_table.shape
    assert m % n_seq == 0 and seq_len % (m // n_seq) == 0 and (m // n_seq) % CHUNK == 0
    tm = m // n_seq
    assert n_pages % GROUP == 0
    assert n_pages * WIDTH * PAGE * 4 <= VMEM_LIMIT // 2
    per_seq = seq_len // tm
    n_chunks = REST_WIDTH // PROJ_CHUNK
    row = lambda i, pt: (i, 0)
    mem = pl.BlockSpec((1, N_MEM, WIDTH), lambda i, pt: (i // per_seq, 0, 0))
    tok = lambda shape: pl.BlockSpec((1,) + shape, lambda b, pt: (b, 0, 0))
    hbm = pl.BlockSpec(memory_space=pl.ANY)
    grid_spec = pltpu.PrefetchScalarGridSpec(
        num_scalar_prefetch=1, grid=(n_seq,),
        in_specs=[pl.BlockSpec((tm, D_MODEL), row), pl.BlockSpec((tm, WIDTH), row),
                  _resident((1, D_MODEL)), _resident((n_chunks, D_MODEL, PROJ_CHUNK)),
                  _resident((1, WIDTH)), _resident((1, WIDTH)),
                  _resident((WIDTH // LANES, 2 * CHUNK, CHUNK)), _resident((CHUNK, WIDTH)),
                  mem, mem,
                  _resident((3, WIDTH, D_MODEL)), _resident((D_MODEL, D_MODEL)), _resident((1, D_MODEL)),
                  tok((1, WIDTH)), tok((1, WIDTH)), tok((1, WIDTH)), tok((FOX_HEADS, 1)),
                  hbm, hbm, hbm],
        out_specs=[pl.BlockSpec((tm, D_MODEL), row), tok((1, WIDTH))],
        scratch_shapes=[pltpu.VMEM((3, tm, WIDTH), BF16),
                        pltpu.VMEM((n_chunks, tm, PROJ_CHUNK), F32),
                        pltpu.VMEM((n_pages, WIDTH, PAGE), F32),
                        pltpu.VMEM((2, n_pages, FOX_HEADS, PAGE), F32),
                        pltpu.VMEM((n_pages, FOX_HEADS, PAGE), F32),
                        pltpu.VMEM((WIDTH, PAGE), F32),
                        pltpu.SemaphoreType.DMA((n_pages,)),
                        pltpu.SemaphoreType.DMA((2,))])
    return pl.pallas_call(
        functools.partial(_finish_decode_kernel, tm=tm, n_pages=n_pages, n_seq=n_seq),
        grid_spec=grid_spec,
        out_shape=[jax.ShapeDtypeStruct((m, D_MODEL), F32), jax.ShapeDtypeStruct((n_seq, 1, WIDTH), F32)],
        compiler_params=_params(1), name="finish_and_decode",
    )(page_table, x2d, fox2d, g_pre, w_chunks, ln_g, ln_b, w_sp, b_sp, mkb, mvb, w_br, w_out, g_post,
      q, k_new, v_new, c_new, cache_kt, cache_vt, cache_lft)


def _mem_decode_kernel(q_ref, k_ref, v_ref, o_ref):
    n_tok, n_flat, _ = k_ref.shape
    sub = lax.broadcasted_iota(jnp.int32, (2 * MEM_HEADS, n_flat), 0)
    lane = lax.broadcasted_iota(jnp.int32, (2 * MEM_HEADS, n_flat), 1)
    own = (lane % MEM_HEADS) == sub
    for i in range(n_tok):
        q = q_ref[i]
        q_rows = jnp.concatenate([q, jnp.zeros_like(q)], axis=0).astype(BF16)
        s = _dot_nt(q_rows, k_ref[i].astype(BF16)) * MEM_SCALE
        m = jnp.max(jnp.where(own, s, -1e30), axis=-1, keepdims=True)
        p = jnp.where(own, jnp.exp(s - m), 0.0)
        o = _dot(p.astype(BF16), v_ref[i].astype(BF16))
        o_ref[i] = o[:MEM_HEADS] / jnp.sum(p, axis=-1, keepdims=True)[:MEM_HEADS]


def _mem_decode(q, mem_k, mem_v):
    n_seq, n_flat, _ = mem_k.shape
    n_tok = MEM_DECODE_TOKENS if n_seq % MEM_DECODE_TOKENS == 0 else 1
    tok = pl.BlockSpec((n_tok, MEM_HEADS, MEM_DIM), lambda b: (b, 0, 0))
    mem = pl.BlockSpec((n_tok, n_flat, MEM_DIM), lambda b: (b, 0, 0))
    return pl.pallas_call(
        _mem_decode_kernel, grid=(n_seq // n_tok,), in_specs=[tok, mem, mem], out_specs=tok,
        out_shape=jax.ShapeDtypeStruct((n_seq, MEM_HEADS, MEM_DIM), F32),
        compiler_params=_params(1), name="mem_decode",
    )(q, mem_k, mem_v)


def _sample_finish_kernel(x_ref, z_ref, fox_ref, mem_ref, lng_ref, lnb_ref, ws0_ref, bs0_ref,
                          wbr_ref, wout_ref, gpost_ref, y_ref, av_ref, br_ref):
    col = lambda c0, width=WIDTH: z_ref[:, c0:c0 + width]
    v = _layer_norm(_gelu(col(R_VA)), lng_ref[...], lnb_ref[...])
    av_ref[...] = v
    mixed = ws0_ref[...] * v + bs0_ref[...]
    br_ref[0] = (_gelu(col(R_UA)) * mixed * _silu(col(R_ZA))).astype(BF16)
    br_ref[1] = (fox_ref[...] * _silu(col(R_ZB))).astype(BF16)
    br_ref[2] = (mem_ref[...] * _silu(col(R_ZC))).astype(BF16)
    y = _gated_merge(br_ref, wbr_ref, wout_ref, lambda n: col(R_GATE + n * D_MODEL, D_MODEL))
    y_ref[...] = x_ref[...] + _rms(y, gpost_ref[...])


def _sample_finish(x2d, z, fox, mem, ln_g, ln_b, ws0, bs0, w_br, w_out, g_post):
    m = x2d.shape[0]
    args = (x2d, z, fox, mem, ln_g, ln_b, ws0, bs0, w_br, w_out, g_post)
    return pl.pallas_call(
        _sample_finish_kernel, grid=(1,),
        in_specs=[_resident(a.shape) for a in args],
        out_specs=[pl.BlockSpec((m, D_MODEL), lambda i: (0, 0)), pl.BlockSpec((m, WIDTH), lambda i: (0, 0))],
        out_shape=[jax.ShapeDtypeStruct((m, D_MODEL), F32), jax.ShapeDtypeStruct((m, WIDTH), F32)],
        scratch_shapes=[pltpu.VMEM((3, m, WIDTH), BF16)],
        compiler_params=_params(1), name="sample_finish",
    )(*args)


def kernel(x_prompt, x_sample, mem_prompt, cache_k, cache_v, cache_logf, cache_mem_k, cache_mem_v,
           page_table, g_pre, g_post, g_mem, w_in, b_forget, ln_v_g, ln_v_b, w_spatial, b_spatial,
           w_mem_kv, w_branch, w_out):
    depth = g_pre.shape[0]
    assert depth == 1
    n_seq, seq_len, _ = x_prompt.shape
    n_dec, dec_len, _ = x_sample.shape
    assert dec_len == 1
    n_phys = cache_k.shape[1]

    w = w_in[0]
    o_f = 7 * WIDTH
    w_qk = w[:, 3 * WIDTH:5 * WIDTH].astype(BF16)
    w_kvt = w[:, 4 * WIDTH:6 * WIDTH].T.astype(BF16)
    w_ft = jnp.zeros((16, D_MODEL), F32).at[:FOX_HEADS].set(w[:, o_f:o_f + FOX_HEADS].T).astype(BF16)
    w_rest = jnp.concatenate([w[:, :3 * WIDTH], w[:, 6 * WIDTH:o_f], w[:, o_f + FOX_HEADS:]], axis=1).astype(BF16)
    w_chunks = w_rest.reshape(D_MODEL, REST_WIDTH // PROJ_CHUNK, PROJ_CHUNK).transpose(1, 0, 2)
    b_f = b_forget[0].reshape(FOX_HEADS, 1)
    gpre, gpost, gmem = g_pre[0][None], g_post[0][None], g_mem[0][None]
    ln_g, ln_b = ln_v_g[0][None], ln_v_b[0][None]
    w_sp = w_spatial[0].reshape(WIDTH // LANES, 2 * CHUNK, CHUNK)
    b_sp = jnp.repeat(b_spatial[0].T, WIDTH // 8, axis=1)
    ws0 = jnp.repeat(w_spatial[0][:, 0, 0], WIDTH // 8)[None]
    bs0 = b_sp[:1]
    w_br = w_branch[0].astype(BF16)
    w_o = w_out[0].astype(BF16)
    w_mkv = w_mem_kv[0].astype(BF16)

    m = n_seq * seq_len
    xp = x_prompt.reshape(m, D_MODEL)
    tq = min(QKV_TILE, seq_len)
    q, kb, vtb, kt, vt, lft = _qkv_project(xp, n_seq, seq_len, gpre, w_qk, w_kvt, w_ft, b_f, tq)
    pieces = _neg_cumsum(lft)
    decay = jnp.pad(pieces.transpose(0, 2, 1), ((0, 0), (0, 0), (0, LANES - 3 * FOX_HEADS))).astype(BF16)
    t_attn = min(ATTN_TILE, seq_len)
    fox = _fox_attention(q.reshape(n_seq, seq_len, WIDTH), kb.reshape(n_seq, seq_len, WIDTH), vtb, decay, t_attn)
    mk, mv, mkb, mvb = _mem_kv(mem_prompt, gmem, w_mkv)

    xs = x_sample.reshape(n_dec, D_MODEL)
    qs, kbs, _, kts, vts, lfts = _qkv_project(xs, 1, n_dec, gpre, w_qk, w_kvt, w_ft, b_f, n_dec)
    logf_s = lfts[0].T
    zs = _norm_project(xs, gpre, w_chunks)

    tok = lambda a: a.astype(F32).reshape(n_dec, 1, WIDTH)
    pages_t = lambda c: c[0].transpose(0, 2, 3, 1).reshape(n_phys, WIDTH, PAGE)
    yp, fox_s = _finish_and_decode(
        xp, fox.reshape(m, WIDTH), seq_len, gpre, w_chunks, ln_g, ln_b, w_sp, b_sp, mkb, mvb, w_br, w_o, gpost,
        page_table, tok(qs), tok(kbs), tok(vts[0].T), logf_s.reshape(n_dec, FOX_HEADS, 1),
        pages_t(cache_k), pages_t(cache_v), cache_logf[0].transpose(0, 2, 1))
    mem_rows = lambda c: c.reshape(n_dec, N_MEM * MEM_HEADS, MEM_DIM)
    mem_s = _mem_decode(zs[:, R_QC:R_QC + WIDTH].reshape(n_dec, MEM_HEADS, MEM_DIM),
                        mem_rows(cache_mem_k), mem_rows(cache_mem_v))
    ys, av = _sample_finish(xs, zs, fox_s.reshape(n_dec, WIDTH), mem_s.reshape(n_dec, WIDTH),
                            ln_g, ln_b, ws0, bs0, w_br, w_o, gpost)

    def token_major(t, n, length):
        return t.reshape(1, n, FOX_HEADS, FOX_DIM, length).transpose(0, 1, 4, 2, 3)

    return (yp.reshape(n_seq, seq_len, D_MODEL),
            ys.reshape(n_dec, 1, D_MODEL),
            token_major(kt, n_seq, seq_len), token_major(vt, n_seq, seq_len),
            lft.transpose(0, 2, 1)[None],
            mk.reshape(1, n_seq, N_MEM, MEM_HEADS, MEM_DIM), mv.reshape(1, n_seq, N_MEM, MEM_HEADS, MEM_DIM),
            token_major(kts, 1, n_dec).reshape(1, n_dec, 1, FOX_HEADS, FOX_DIM),
            token_major(vts, 1, n_dec).reshape(1, n_dec, 1, FOX_HEADS, FOX_DIM),
            logf_s.reshape(1, n_dec, 1, FOX_HEADS),
            av.reshape(1, n_dec, 1, WIDTH))
```

```python
import functools

import jax
import jax.numpy as jnp
from jax import lax
from jax.experimental import pallas as pl
from jax.experimental.pallas import tpu as pltpu

F32 = jnp.float32
BF16 = jnp.bfloat16

D_MODEL = 1024
WIDTH = 512
FOX_HEADS = 8
FOX_DIM = 64
MEM_HEADS = 4
MEM_DIM = 128
N_MEM = 256
CHUNK = 128
PAGE = 128
NORM_EPS = 1e-6
FOX_SCALE = FOX_DIM ** -0.5
MEM_SCALE = MEM_DIM ** -0.5
LANES = 128

R_UA, R_VA, R_ZA, R_ZB, R_QC, R_ZC, R_GATE = (i * WIDTH for i in range(7))
REST_WIDTH = R_GATE + 3 * D_MODEL

QKV_TILE = 512
ATTN_TILE = 512
K_GROUP = 32
V_GROUP = 16
PROJ_CHUNK = 1024
MEM_DECODE_TOKENS = 8
VMEM_LIMIT = 56 * 1024 * 1024

_NT = (((1,), (1,)), ((), ()))


def _dot(a, b):
    return jnp.dot(a, b, preferred_element_type=F32)


def _dot_nt(a, b):
    return lax.dot_general(a, b, _NT, preferred_element_type=F32)


def _rms(x, g):
    ms = jnp.mean(x * x, axis=-1, keepdims=True)
    return x * lax.rsqrt(ms + NORM_EPS) * g


def _sigmoid(x):
    return 1.0 / (1.0 + jnp.exp(-x))


def _silu(x):
    return x * _sigmoid(x)


def _gelu(x):
    return jax.nn.gelu(x)


def _log_sigmoid(x):
    return jnp.minimum(x, 0.0) - jnp.log1p(jnp.exp(-jnp.abs(x)))


def _layer_norm(x, g, b):
    mu = jnp.mean(x, axis=-1, keepdims=True)
    xc = x - mu
    var = jnp.mean(xc * xc, axis=-1, keepdims=True)
    return xc * lax.rsqrt(var + NORM_EPS) * g + b


def _split3(x):
    hi = x.astype(BF16)
    r = x - hi.astype(F32)
    mid = r.astype(BF16)
    lo = (r - mid.astype(F32)).astype(BF16)
    return hi, mid, lo


def _params(n_axes):
    return pltpu.CompilerParams(dimension_semantics=("arbitrary",) * n_axes,
                                vmem_limit_bytes=VMEM_LIMIT)


def _resident(shape):
    return pl.BlockSpec(shape, lambda *_: (0,) * len(shape), pipeline_mode=pl.Buffered(1))


def _qkv_kernel(x_ref, g_ref, w_ref, wkvt_ref, wft_ref, bf_ref, q_ref, kb_ref, vtb_ref, kt_ref, vt_ref, lft_ref):
    h = _rms(x_ref[...], g_ref[...]).astype(BF16)
    z = _dot(h, w_ref[...])
    q_ref[...] = (z[:, :WIDTH] * FOX_SCALE).astype(BF16)
    kb_ref[...] = z[:, WIDTH:].astype(BF16)
    zt = _dot_nt(wkvt_ref[...], h)
    kt_ref[0] = zt[:WIDTH]
    vt_ref[0] = zt[WIDTH:]
    vtb_ref[0] = zt[WIDTH:].astype(BF16)
    ft = _dot_nt(wft_ref[...], h)
    lft_ref[0] = _log_sigmoid(ft[:FOX_HEADS] + bf_ref[...])


def _qkv_project(x2d, n_seq, seq_len, g_pre, w_qk, w_kvt, w_ft, b_f, tm):
    m = x2d.shape[0]
    per_seq = seq_len // tm
    row = lambda i: (i, 0)
    chan = lambda rows: pl.BlockSpec((1, rows, tm), lambda i: (i // per_seq, 0, i % per_seq))
    return pl.pallas_call(
        _qkv_kernel,
        grid=(m // tm,),
        in_specs=[pl.BlockSpec((tm, D_MODEL), row),
                  _resident((1, D_MODEL)),
                  _resident((D_MODEL, 2 * WIDTH)),
                  _resident((2 * WIDTH, D_MODEL)),
                  _resident((16, D_MODEL)),
                  _resident((FOX_HEADS, 1))],
        out_specs=[pl.BlockSpec((tm, WIDTH), row)] * 2 + [chan(WIDTH)] * 3 + [chan(FOX_HEADS)],
        out_shape=[jax.ShapeDtypeStruct((m, WIDTH), BF16)] * 2
        + [jax.ShapeDtypeStruct((n_seq, WIDTH, seq_len), BF16)]
        + [jax.ShapeDtypeStruct((n_seq, WIDTH, seq_len), F32)] * 2
        + [jax.ShapeDtypeStruct((n_seq, FOX_HEADS, seq_len), F32)],
        compiler_params=_params(1),
        name="qkv_project",
    )(x2d, g_pre, w_qk, w_kvt, w_ft, b_f)


def _neg_cumsum_kernel(lft_ref, o_ref):
    x = lft_ref[0]
    pos = lax.broadcasted_iota(jnp.int32, x.shape, 1)
    shift = 1
    while shift < x.shape[-1]:
        x = x + jnp.where(pos >= shift, pltpu.roll(x, shift, 1), 0.0)
        shift *= 2
    hi, mid, lo = _split3(-x)
    o_ref[0] = jnp.concatenate([hi.astype(F32), mid.astype(F32), lo.astype(F32)], axis=0)


def _neg_cumsum(lft):
    n_seq, heads, seq_len = lft.shape
    return pl.pallas_call(
        _neg_cumsum_kernel, grid=(n_seq,),
        in_specs=[pl.BlockSpec((1, heads, seq_len), lambda b: (b, 0, 0))],
        out_specs=pl.BlockSpec((1, 3 * heads, seq_len), lambda b: (b, 0, 0)),
        out_shape=jax.ShapeDtypeStruct((n_seq, 3 * heads, seq_len), F32),
        compiler_params=_params(1), name="decay_cumsum",
    )(lft)


def _fox_kernel(q_ref, k_ref, vt_ref, nc_ref, o_ref, m_ref, l_ref, acc_ref, s_ref, *, t):
    qi = pl.program_id(1)
    key = lax.broadcasted_iota(jnp.int32, (t, t), 0)
    qry = lax.broadcasted_iota(jnp.int32, (t, t), 1)
    causal = key <= qry
    lane = lax.broadcasted_iota(jnp.int32, (t, LANES), 1)
    lower = lane < FOX_DIM

    m_ref[...] = jnp.full(m_ref.shape, -jnp.inf, F32)
    l_ref[...] = jnp.zeros(l_ref.shape, F32)
    acc_ref[...] = jnp.zeros(acc_ref.shape, F32)

    def q_aug(h):
        lanes = slice((h // 2) * LANES, (h // 2 + 1) * LANES)
        q_blk = q_ref[0, :, lanes]
        q_h = jnp.where(lower if h % 2 == 0 else jnp.logical_not(lower), q_blk, jnp.zeros_like(q_blk))
        pick = (lane % FOX_HEADS == h) & (lane < 3 * FOX_HEADS)
        return jnp.concatenate([q_h, jnp.where(pick, 1.0, 0.0).astype(BF16)], axis=1)

    def kv_step(kv, masked):
        off = pl.multiple_of(kv * t, t)
        decay = nc_ref[0, pl.ds(off, t), :]

        def scores_to(slot, h):
            lanes = slice((h // 2) * LANES, (h // 2 + 1) * LANES)
            k_aug = jnp.concatenate([k_ref[0, pl.ds(off, t), lanes], decay], axis=1)
            s_ref[slot] = _dot_nt(k_aug, q_aug(h))

        scores_to(0, 0)
        for h in range(FOX_HEADS):
            if h + 1 < FOX_HEADS:
                scores_to((h + 1) % 2, h + 1)
            s = s_ref[h % 2]
            if masked:
                s = jnp.where(causal, s, -jnp.inf)
            m = m_ref[h:h + 1, :]
            m_new = jnp.maximum(m, jnp.max(s, axis=0, keepdims=True))
            p = jnp.exp(s - m_new)
            corr = jnp.exp(m - m_new)
            m_ref[h:h + 1, :] = m_new
            l_ref[h:h + 1, :] = l_ref[h:h + 1, :] * corr + jnp.sum(p, axis=0, keepdims=True)
            v_t = vt_ref[0, h * FOX_DIM:(h + 1) * FOX_DIM, pl.ds(off, t)]
            acc_ref[h] = acc_ref[h] * corr + _dot(v_t, p.astype(BF16))

    def body(kv, carry):
        kv_step(kv, masked=False)
        return carry

    lax.fori_loop(0, qi, body, 0)
    kv_step(qi, masked=True)

    for j in range(WIDTH // LANES):
        pair = jnp.concatenate([acc_ref[2 * j] / l_ref[2 * j:2 * j + 1, :],
                                acc_ref[2 * j + 1] / l_ref[2 * j + 1:2 * j + 2, :]], axis=0)
        o_ref[0, :, j * LANES:(j + 1) * LANES] = pair.T


def _fox_attention(q, k, v_t, decay, t):
    n_seq, seq_len, _ = q.shape
    tile = pl.BlockSpec((1, t, WIDTH), lambda b, i: (b, i, 0))
    return pl.pallas_call(
        functools.partial(_fox_kernel, t=t),
        grid=(n_seq, seq_len // t),
        in_specs=[tile,
                  pl.BlockSpec((1, seq_len, WIDTH), lambda b, i: (b, 0, 0)),
                  pl.BlockSpec((1, WIDTH, seq_len), lambda b, i: (b, 0, 0)),
                  pl.BlockSpec((1, seq_len, LANES), lambda b, i: (b, 0, 0))],
        out_specs=tile,
        out_shape=jax.ShapeDtypeStruct((n_seq, seq_len, WIDTH), F32),
        scratch_shapes=[pltpu.VMEM((FOX_HEADS, t), F32), pltpu.VMEM((FOX_HEADS, t), F32),
                        pltpu.VMEM((FOX_HEADS, FOX_DIM, t), F32), pltpu.VMEM((2, t, t), F32)],
        compiler_params=_params(2), name="fox_attention",
    )(q, k, v_t, decay)


def _mem_kv_kernel(mem_ref, g_ref, w_ref, k_ref, v_ref, kb_ref, vb_ref):
    h = _rms(mem_ref[0], g_ref[...]).astype(BF16)
    kv = _dot(h, w_ref[...])
    k_ref[0] = kv[:, :WIDTH]
    v_ref[0] = kv[:, WIDTH:]
    kb_ref[0] = kv[:, :WIDTH].astype(BF16)
    vb_ref[0] = kv[:, WIDTH:].astype(BF16)


def _mem_kv(mem, g_mem, w_mem_kv):
    n_seq = mem.shape[0]
    blk = pl.BlockSpec((1, N_MEM, WIDTH), lambda b: (b, 0, 0))
    return pl.pallas_call(
        _mem_kv_kernel, grid=(n_seq,),
        in_specs=[pl.BlockSpec((1, N_MEM, D_MODEL), lambda b: (b, 0, 0)),
                  _resident((1, D_MODEL)), _resident((D_MODEL, 2 * WIDTH))],
        out_specs=[blk] * 4,
        out_shape=[jax.ShapeDtypeStruct((n_seq, N_MEM, WIDTH), F32)] * 2
        + [jax.ShapeDtypeStruct((n_seq, N_MEM, WIDTH), BF16)] * 2,
        compiler_params=_params(1), name="mem_kv",
    )(mem, g_mem, w_mem_kv)


def _gated_merge(branch_ref, wbr_ref, wout_ref, gate_fn):
    merged = None
    for n in range(3):
        proj = _dot(branch_ref[n], wbr_ref[n])
        term = _sigmoid(gate_fn(n)) * proj
        merged = term if merged is None else merged + term
    return _dot(merged.astype(BF16), wout_ref[...])


def _finish_tile(x_ref, fox_ref, z_ref, lng_ref, lnb_ref, wsp_ref, bsp_ref,
                 mk_ref, mv_ref, wbr_ref, wout_ref, gpost_ref, y_ref, br_ref, *, tm):
    def proj(c0, width=WIDTH):
        lane0 = c0 % PROJ_CHUNK
        return z_ref[c0 // PROJ_CHUNK, :, lane0:lane0 + width]

    u = _gelu(proj(R_UA))
    v = _layer_norm(_gelu(proj(R_VA)), lng_ref[...], lnb_ref[...]).astype(BF16)
    gate_a = _silu(proj(R_ZA))
    n_chunk = tm // CHUNK
    r2 = lax.broadcasted_iota(jnp.int32, (2 * CHUNK, CHUNK), 0)
    c2 = lax.broadcasted_iota(jnp.int32, (2 * CHUNK, CHUNK), 1)
    tril = (r2 & (CHUNK - 1)) >= c2
    lower = lax.broadcasted_iota(jnp.int32, (CHUNK, LANES), 1) < (LANES // 2)
    for j in range(WIDTH // LANES):
        lanes = slice(j * LANES, (j + 1) * LANES)
        w_pair = jnp.where(tril, wsp_ref[j], 0.0).astype(BF16)
        rhs = jnp.concatenate([v[c * CHUNK:(c + 1) * CHUNK, lanes] for c in range(n_chunk)], axis=1)
        mix = _dot(w_pair, rhs)
        for c in range(n_chunk):
            blk = mix[:, c * LANES:(c + 1) * LANES]
            mixed = jnp.where(lower, blk[:CHUNK], blk[CHUNK:]) + bsp_ref[:, lanes]
            rows = slice(c * CHUNK, (c + 1) * CHUNK)
            br_ref[0, rows, lanes] = (u[rows, lanes] * mixed * gate_a[rows, lanes]).astype(BF16)

    br_ref[1] = (fox_ref[...] * _silu(proj(R_ZB))).astype(BF16)

    qc = proj(R_QC)
    gate_c = _silu(proj(R_ZC))
    for hh in range(MEM_HEADS):
        lanes = slice(hh * MEM_DIM, (hh + 1) * MEM_DIM)
        s = _dot_nt(qc[:, lanes].astype(BF16), mk_ref[0, :, lanes]) * MEM_SCALE
        p = jnp.exp(s - jnp.max(s, axis=-1, keepdims=True))
        o = _dot(p.astype(BF16), mv_ref[0, :, lanes]) / jnp.sum(p, axis=-1, keepdims=True)
        br_ref[2, :, lanes] = (o * gate_c[:, lanes]).astype(BF16)

    y = _gated_merge(br_ref, wbr_ref, wout_ref, lambda n: proj(R_GATE + n * D_MODEL, D_MODEL))
    y_ref[...] = x_ref[...] + _rms(y, gpost_ref[...])


def _norm_proj_kernel(x_ref, g_ref, w_ref, z_ref):
    z_ref[...] = _dot(_rms(x_ref[...], g_ref[...]).astype(BF16), w_ref[0])


def _norm_project(x2d, g_pre, w_chunks):
    m = x2d.shape[0]
    n_chunks, _, tn = w_chunks.shape
    return pl.pallas_call(
        _norm_proj_kernel, grid=(n_chunks,),
        in_specs=[_resident((m, D_MODEL)), _resident((1, D_MODEL)),
                  pl.BlockSpec((1, D_MODEL, tn), lambda j: (j, 0, 0))],
        out_specs=pl.BlockSpec((m, tn), lambda j: (0, j)),
        out_shape=jax.ShapeDtypeStruct((m, n_chunks * tn), F32),
        compiler_params=_params(1), name="sample_rest_project",
    )(x2d, g_pre, w_chunks)


def _decode_sequence(b, pt_ref, q_ref, kn_ref, vn_ref, cn_ref, ck_ref, cv_ref, clf_ref, o_ref,
                     ring, lfbuf, sc, acc_ref, sem_ring, sem_lf, *, n_pages, n_seq, while_values_load):
    k_group_size = min(K_GROUP, n_pages)
    v_group_size = min(V_GROUP, n_pages)

    def page_copy(src, seq, p, slot):
        return pltpu.make_async_copy(src.at[pt_ref[seq, p]], ring.at[slot], sem_ring.at[slot])

    def lf_copy(seq, p, half):
        return pltpu.make_async_copy(clf_ref.at[pt_ref[seq, p]], lfbuf.at[half, p], sem_lf.at[half])

    def start_group(src, seq, grp, size):
        for u in range(size):
            page_copy(src, seq, grp * size + u, grp * size + u).start()

    def wait_group(src, grp, size):
        for u in range(size):
            page_copy(src, b, grp * size + u, grp * size + u).wait()

    @pl.when(b == 0)
    def _():
        for p in range(n_pages):
            lf_copy(0, p, 0).start()
        start_group(ck_ref, 0, 0, n_pages)

    @pl.when(b + 1 < n_seq)
    def _():
        for p in range(n_pages):
            lf_copy(b + 1, p, (b + 1) % 2).start()

    half = b % 2
    for p in range(n_pages):
        lf_copy(b, p, half).wait()

    lf = lfbuf[half].reshape(n_pages * FOX_HEADS, PAGE)
    r = lax.broadcasted_iota(jnp.int32, (PAGE, PAGE), 0)
    c = lax.broadcasted_iota(jnp.int32, (PAGE, PAGE), 1)
    after = (r > c).astype(BF16)
    ones = jnp.ones((PAGE, PAGE), BF16)
    parts = _split3(lf)
    within = sum(_dot(x, after) for x in parts).reshape(n_pages, FOX_HEADS, PAGE)
    total = sum(_dot(x, ones) for x in parts).reshape(n_pages, FOX_HEADS, PAGE)
    run = jnp.broadcast_to(cn_ref[0], (FOX_HEADS, PAGE))
    for p in reversed(range(n_pages)):
        sc[p] = within[p] + run
        run = run + total[p]

    def head_rows(h):
        return slice(h * FOX_DIM, (h + 1) * FOX_DIM)

    r8 = lax.broadcasted_iota(jnp.int32, (FOX_HEADS, WIDTH), 0)
    c8 = lax.broadcasted_iota(jnp.int32, (FOX_HEADS, WIDTH), 1)
    head_mask = (c8 // FOX_DIM) == r8

    def on_lanes(col):
        return jnp.sum(jnp.where(head_mask, jnp.broadcast_to(col, head_mask.shape), 0.0), axis=0, keepdims=True)

    q_f32 = jnp.where(head_mask, jnp.broadcast_to(q_ref[0], head_mask.shape), 0.0)
    q_rows = q_f32.astype(BF16)
    acc_ref[...] = jnp.zeros((WIDTH, PAGE), F32)

    def k_group(grp, carry):
        wait_group(ck_ref, grp, k_group_size)
        for u in range(k_group_size):
            p = grp * k_group_size + u
            sc[p] = sc[p] + _dot(q_rows, ring[p].astype(BF16))
        start_group(cv_ref, b, grp, k_group_size)
        return carry

    lax.fori_loop(0, n_pages // k_group_size, k_group, 0)

    s_all = sc[...]
    s_new = jnp.sum(q_f32 * kn_ref[0], axis=-1, keepdims=True)
    m = jnp.maximum(jnp.max(jnp.max(s_all, axis=0), axis=-1, keepdims=True), s_new)
    p_all = jnp.exp(s_all - m[None])
    p_new = jnp.exp(s_new - m)
    denom = jnp.sum(jnp.sum(p_all, axis=0), axis=-1, keepdims=True) + p_new
    sc[...] = p_all

    def v_group(grp, carry):
        wait_group(cv_ref, grp, v_group_size)
        probs = [sc[grp * v_group_size + u] for u in range(v_group_size)]
        for h in range(FOX_HEADS):
            a = acc_ref[head_rows(h), :]
            for u in range(v_group_size):
                a = a + ring[grp * v_group_size + u, head_rows(h), :] * probs[u][h:h + 1, :]
            acc_ref[head_rows(h), :] = a

        @pl.when(b + 1 < n_seq)
        def _():
            start_group(ck_ref, b + 1, grp, v_group_size)

        return carry

    while_values_load()
    lax.fori_loop(0, n_pages // v_group_size, v_group, 0)
    out = jnp.sum(acc_ref[...].T, axis=0, keepdims=True) + on_lanes(p_new) * vn_ref[0]
    o_ref[0] = out / on_lanes(denom)


def _finish_decode_kernel(pt_ref, x_ref, fox_ref, gpre_ref, w_ref, lng_ref, lnb_ref, wsp_ref, bsp_ref,
                          mk_ref, mv_ref, wbr_ref, wout_ref, gpost_ref,
                          q_ref, kn_ref, vn_ref, cn_ref, ck_ref, cv_ref, clf_ref,
                          y_ref, o_ref,
                          br_ref, z_ref, ring, lfbuf, sc, acc_ref, sem_ring, sem_lf,
                          *, tm, n_pages, n_seq):
    def project():
        h = _rms(x_ref[...], gpre_ref[...]).astype(BF16)
        for c in range(REST_WIDTH // PROJ_CHUNK):
            z_ref[c] = _dot(h, w_ref[c])

    _decode_sequence(pl.program_id(0), pt_ref, q_ref, kn_ref, vn_ref, cn_ref, ck_ref, cv_ref, clf_ref, o_ref,
                     ring, lfbuf, sc, acc_ref, sem_ring, sem_lf, n_pages=n_pages, n_seq=n_seq,
                     while_values_load=project)
    _finish_tile(x_ref, fox_ref, z_ref, lng_ref, lnb_ref, wsp_ref, bsp_ref, mk_ref, mv_ref,
                 wbr_ref, wout_ref, gpost_ref, y_ref, br_ref, tm=tm)


def _finish_and_decode(x2d, fox2d, seq_len, g_pre, w_chunks, ln_g, ln_b, w_sp, b_sp, mkb, mvb, w_br, w_out, g_post,
                       page_table, q, k_new, v_new, c_new, cache_kt, cache_vt, cache_lft):
    m = x2d.shape[0]
    n_seq, n_pages = page_table.shape
    assert m % n_seq == 0 and seq_len % (m // n_seq) == 0 and (m // n_seq) % CHUNK == 0
    tm = m // n_seq
    assert n_pages % min(K_GROUP, n_pages) == 0 and n_pages % min(V_GROUP, n_pages) == 0
    assert n_pages * WIDTH * PAGE * 4 <= VMEM_LIMIT // 2
    per_seq = seq_len // tm
    n_chunks = REST_WIDTH // PROJ_CHUNK
    row = lambda i, pt: (i, 0)
    mem = pl.BlockSpec((1, N_MEM, WIDTH), lambda i, pt: (i // per_seq, 0, 0))
    tok = lambda shape: pl.BlockSpec((1,) + shape, lambda b, pt: (b, 0, 0))
    hbm = pl.BlockSpec(memory_space=pl.ANY)
    grid_spec = pltpu.PrefetchScalarGridSpec(
        num_scalar_prefetch=1, grid=(n_seq,),
        in_specs=[pl.BlockSpec((tm, D_MODEL), row), pl.BlockSpec((tm, WIDTH), row),
                  _resident((1, D_MODEL)), _resident((n_chunks, D_MODEL, PROJ_CHUNK)),
                  _resident((1, WIDTH)), _resident((1, WIDTH)),
                  _resident((WIDTH // LANES, 2 * CHUNK, CHUNK)), _resident((CHUNK, WIDTH)),
                  mem, mem,
                  _resident((3, WIDTH, D_MODEL)), _resident((D_MODEL, D_MODEL)), _resident((1, D_MODEL)),
                  tok((1, WIDTH)), tok((1, WIDTH)), tok((1, WIDTH)), tok((FOX_HEADS, 1)),
                  hbm, hbm, hbm],
        out_specs=[pl.BlockSpec((tm, D_MODEL), row), tok((1, WIDTH))],
        scratch_shapes=[pltpu.VMEM((3, tm, WIDTH), BF16),
                        pltpu.VMEM((n_chunks, tm, PROJ_CHUNK), F32),
                        pltpu.VMEM((n_pages, WIDTH, PAGE), F32),
                        pltpu.VMEM((2, n_pages, FOX_HEADS, PAGE), F32),
                        pltpu.VMEM((n_pages, FOX_HEADS, PAGE), F32),
                        pltpu.VMEM((WIDTH, PAGE), F32),
                        pltpu.SemaphoreType.DMA((n_pages,)),
                        pltpu.SemaphoreType.DMA((2,))])
    return pl.pallas_call(
        functools.partial(_finish_decode_kernel, tm=tm, n_pages=n_pages, n_seq=n_seq),
        grid_spec=grid_spec,
        out_shape=[jax.ShapeDtypeStruct((m, D_MODEL), F32), jax.ShapeDtypeStruct((n_seq, 1, WIDTH), F32)],
        compiler_params=_params(1), name="finish_and_decode",
    )(page_table, x2d, fox2d, g_pre, w_chunks, ln_g, ln_b, w_sp, b_sp, mkb, mvb, w_br, w_out, g_post,
      q, k_new, v_new, c_new, cache_kt, cache_vt, cache_lft)


def _mem_decode_kernel(q_ref, k_ref, v_ref, o_ref):
    n_tok, n_flat, _ = k_ref.shape
    sub = lax.broadcasted_iota(jnp.int32, (2 * MEM_HEADS, n_flat), 0)
    lane = lax.broadcasted_iota(jnp.int32, (2 * MEM_HEADS, n_flat), 1)
    own = (lane % MEM_HEADS) == sub
    for i in range(n_tok):
        q = q_ref[i]
        q_rows = jnp.concatenate([q, jnp.zeros_like(q)], axis=0).astype(BF16)
        s = _dot_nt(q_rows, k_ref[i].astype(BF16)) * MEM_SCALE
        m = jnp.max(jnp.where(own, s, -1e30), axis=-1, keepdims=True)
        p = jnp.where(own, jnp.exp(s - m), 0.0)
        o = _dot(p.astype(BF16), v_ref[i].astype(BF16))
        o_ref[i] = o[:MEM_HEADS] / jnp.sum(p, axis=-1, keepdims=True)[:MEM_HEADS]


def _mem_decode(q, mem_k, mem_v):
    n_seq, n_flat, _ = mem_k.shape
    n_tok = MEM_DECODE_TOKENS if n_seq % MEM_DECODE_TOKENS == 0 else 1
    tok = pl.BlockSpec((n_tok, MEM_HEADS, MEM_DIM), lambda b: (b, 0, 0))
    mem = pl.BlockSpec((n_tok, n_flat, MEM_DIM), lambda b: (b, 0, 0))
    return pl.pallas_call(
        _mem_decode_kernel, grid=(n_seq // n_tok,), in_specs=[tok, mem, mem], out_specs=tok,
        out_shape=jax.ShapeDtypeStruct((n_seq, MEM_HEADS, MEM_DIM), F32),
        compiler_params=_params(1), name="mem_decode",
    )(q, mem_k, mem_v)


def _sample_finish_kernel(x_ref, z_ref, fox_ref, mem_ref, lng_ref, lnb_ref, ws0_ref, bs0_ref,
                          wbr_ref, wout_ref, gpost_ref, y_ref, av_ref, br_ref):
    col = lambda c0, width=WIDTH: z_ref[:, c0:c0 + width]
    v = _layer_norm(_gelu(col(R_VA)), lng_ref[...], lnb_ref[...])
    av_ref[...] = v
    mixed = ws0_ref[...] * v + bs0_ref[...]
    br_ref[0] = (_gelu(col(R_UA)) * mixed * _silu(col(R_ZA))).astype(BF16)
    br_ref[1] = (fox_ref[...] * _silu(col(R_ZB))).astype(BF16)
    br_ref[2] = (mem_ref[...] * _silu(col(R_ZC))).astype(BF16)
    y = _gated_merge(br_ref, wbr_ref, wout_ref, lambda n: col(R_GATE + n * D_MODEL, D_MODEL))
    y_ref[...] = x_ref[...] + _rms(y, gpost_ref[...])


def _sample_finish(x2d, z, fox, mem, ln_g, ln_b, ws0, bs0, w_br, w_out, g_post):
    m = x2d.shape[0]
    args = (x2d, z, fox, mem, ln_g, ln_b, ws0, bs0, w_br, w_out, g_post)
    return pl.pallas_call(
        _sample_finish_kernel, grid=(1,),
        in_specs=[_resident(a.shape) for a in args],
        out_specs=[pl.BlockSpec((m, D_MODEL), lambda i: (0, 0)), pl.BlockSpec((m, WIDTH), lambda i: (0, 0))],
        out_shape=[jax.ShapeDtypeStruct((m, D_MODEL), F32), jax.ShapeDtypeStruct((m, WIDTH), F32)],
        scratch_shapes=[pltpu.VMEM((3, m, WIDTH), BF16)],
        compiler_params=_params(1), name="sample_finish",
    )(*args)


def kernel(x_prompt, x_sample, mem_prompt, cache_k, cache_v, cache_logf, cache_mem_k, cache_mem_v,
           page_table, g_pre, g_post, g_mem, w_in, b_forget, ln_v_g, ln_v_b, w_spatial, b_spatial,
           w_mem_kv, w_branch, w_out):
    depth = g_pre.shape[0]
    assert depth == 1
    n_seq, seq_len, _ = x_prompt.shape
    n_dec, dec_len, _ = x_sample.shape
    assert dec_len == 1
    n_phys = cache_k.shape[1]

    w = w_in[0]
    o_f = 7 * WIDTH
    w_qk = w[:, 3 * WIDTH:5 * WIDTH].astype(BF16)
    w_kvt = w[:, 4 * WIDTH:6 * WIDTH].T.astype(BF16)
    w_ft = jnp.zeros((16, D_MODEL), F32).at[:FOX_HEADS].set(w[:, o_f:o_f + FOX_HEADS].T).astype(BF16)
    w_rest = jnp.concatenate([w[:, :3 * WIDTH], w[:, 6 * WIDTH:o_f], w[:, o_f + FOX_HEADS:]], axis=1).astype(BF16)
    w_chunks = w_rest.reshape(D_MODEL, REST_WIDTH // PROJ_CHUNK, PROJ_CHUNK).transpose(1, 0, 2)
    b_f = b_forget[0].reshape(FOX_HEADS, 1)
    gpre, gpost, gmem = g_pre[0][None], g_post[0][None], g_mem[0][None]
    ln_g, ln_b = ln_v_g[0][None], ln_v_b[0][None]
    w_sp = w_spatial[0].reshape(WIDTH // LANES, 2 * CHUNK, CHUNK)
    b_sp = jnp.repeat(b_spatial[0].T, WIDTH // 8, axis=1)
    ws0 = jnp.repeat(w_spatial[0][:, 0, 0], WIDTH // 8)[None]
    bs0 = b_sp[:1]
    w_br = w_branch[0].astype(BF16)
    w_o = w_out[0].astype(BF16)
    w_mkv = w_mem_kv[0].astype(BF16)

    m = n_seq * seq_len
    xp = x_prompt.reshape(m, D_MODEL)
    tq = min(QKV_TILE, seq_len)
    q, kb, vtb, kt, vt, lft = _qkv_project(xp, n_seq, seq_len, gpre, w_qk, w_kvt, w_ft, b_f, tq)
    pieces = _neg_cumsum(lft)
    decay = jnp.pad(pieces.transpose(0, 2, 1), ((0, 0), (0, 0), (0, LANES - 3 * FOX_HEADS))).astype(BF16)
    t_attn = min(ATTN_TILE, seq_len)
    fox = _fox_attention(q.reshape(n_seq, seq_len, WIDTH), kb.reshape(n_seq, seq_len, WIDTH), vtb, decay, t_attn)
    mk, mv, mkb, mvb = _mem_kv(mem_prompt, gmem, w_mkv)

    xs = x_sample.reshape(n_dec, D_MODEL)
    qs, kbs, _, kts, vts, lfts = _qkv_project(xs, 1, n_dec, gpre, w_qk, w_kvt, w_ft, b_f, n_dec)
    logf_s = lfts[0].T
    zs = _norm_project(xs, gpre, w_chunks)

    tok = lambda a: a.astype(F32).reshape(n_dec, 1, WIDTH)
    pages_t = lambda c: c[0].transpose(0, 2, 3, 1).reshape(n_phys, WIDTH, PAGE)
    yp, fox_s = _finish_and_decode(
        xp, fox.reshape(m, WIDTH), seq_len, gpre, w_chunks, ln_g, ln_b, w_sp, b_sp, mkb, mvb, w_br, w_o, gpost,
        page_table, tok(qs), tok(kbs), tok(vts[0].T), logf_s.reshape(n_dec, FOX_HEADS, 1),
        pages_t(cache_k), pages_t(cache_v), cache_logf[0].transpose(0, 2, 1))
    mem_rows = lambda c: c.reshape(n_dec, N_MEM * MEM_HEADS, MEM_DIM)
    mem_s = _mem_decode(zs[:, R_QC:R_QC + WIDTH].reshape(n_dec, MEM_HEADS, MEM_DIM),
                        mem_rows(cache_mem_k), mem_rows(cache_mem_v))
    ys, av = _sample_finish(xs, zs, fox_s.reshape(n_dec, WIDTH), mem_s.reshape(n_dec, WIDTH),
                            ln_g, ln_b, ws0, bs0, w_br, w_o, gpost)

    def token_major(t, n, length):
        return t.reshape(1, n, FOX_HEADS, FOX_DIM, length).transpose(0, 1, 4, 2, 3)

    return (yp.reshape(n_seq, seq_len, D_MODEL),
            ys.reshape(n_dec, 1, D_MODEL),
            token_major(kt, n_seq, seq_len), token_major(vt, n_seq, seq_len),
            lft.transpose(0, 2, 1)[None],
            mk.reshape(1, n_seq, N_MEM, MEM_HEADS, MEM_DIM), mv.reshape(1, n_seq, N_MEM, MEM_HEADS, MEM_DIM),
            token_major(kts, 1, n_dec).reshape(1, n_dec, 1, FOX_HEADS, FOX_DIM),
            token_major(vts, 1, n_dec).reshape(1, n_dec, 1, FOX_HEADS, FOX_DIM),
            logf_s.reshape(1, n_dec, 1, FOX_HEADS),
            av.reshape(1, n_dec, 1, WIDTH))
```

```python
import functools

import jax
import jax.numpy as jnp
from jax import lax
from jax.experimental import pallas as pl
from jax.experimental.pallas import tpu as pltpu

F32 = jnp.float32
BF16 = jnp.bfloat16

D_MODEL = 1024
WIDTH = 512
FOX_HEADS = 8
FOX_DIM = 64
MEM_HEADS = 4
MEM_DIM = 128
N_MEM = 256
CHUNK = 128
PAGE = 128
NORM_EPS = 1e-6
FOX_SCALE = FOX_DIM ** -0.5
MEM_SCALE = MEM_DIM ** -0.5
LANES = 128

R_UA, R_VA, R_ZA, R_ZB, R_QC, R_ZC, R_GATE = (i * WIDTH for i in range(7))
REST_WIDTH = R_GATE + 3 * D_MODEL

QKV_TILE = 512
ATTN_TILE = 512
GROUP = 16
PROJ_CHUNK = 1024
MEM_DECODE_TOKENS = 4
VMEM_LIMIT = 56 * 1024 * 1024

_NT = (((1,), (1,)), ((), ()))


def _dot(a, b):
    return jnp.dot(a, b, preferred_element_type=F32)


def _dot_nt(a, b):
    return lax.dot_general(a, b, _NT, preferred_element_type=F32)


def _rms(x, g):
    ms = jnp.mean(x * x, axis=-1, keepdims=True)
    return x * lax.rsqrt(ms + NORM_EPS) * g


def _sigmoid(x):
    return 1.0 / (1.0 + jnp.exp(-x))


def _silu(x):
    return x * _sigmoid(x)


def _gelu(x):
    return jax.nn.gelu(x)


def _log_sigmoid(x):
    return jnp.minimum(x, 0.0) - jnp.log1p(jnp.exp(-jnp.abs(x)))


def _layer_norm(x, g, b):
    mu = jnp.mean(x, axis=-1, keepdims=True)
    xc = x - mu
    var = jnp.mean(xc * xc, axis=-1, keepdims=True)
    return xc * lax.rsqrt(var + NORM_EPS) * g + b


def _split3(x):
    hi = x.astype(BF16)
    r = x - hi.astype(F32)
    mid = r.astype(BF16)
    lo = (r - mid.astype(F32)).astype(BF16)
    return hi, mid, lo


def _params(n_axes):
    return pltpu.CompilerParams(dimension_semantics=("arbitrary",) * n_axes,
                                vmem_limit_bytes=VMEM_LIMIT)


def _resident(shape):
    return pl.BlockSpec(shape, lambda *_: (0,) * len(shape), pipeline_mode=pl.Buffered(1))


def _qkv_kernel(x_ref, g_ref, w_ref, wkvt_ref, wft_ref, bf_ref, q_ref, kb_ref, vtb_ref, kt_ref, vt_ref, lft_ref):
    h = _rms(x_ref[...], g_ref[...]).astype(BF16)
    z = _dot(h, w_ref[...])
    q_ref[...] = (z[:, :WIDTH] * FOX_SCALE).astype(BF16)
    kb_ref[...] = z[:, WIDTH:].astype(BF16)
    zt = _dot_nt(wkvt_ref[...], h)
    kt_ref[0] = zt[:WIDTH]
    vt_ref[0] = zt[WIDTH:]
    vtb_ref[0] = zt[WIDTH:].astype(BF16)
    ft = _dot_nt(wft_ref[...], h)
    lft_ref[0] = _log_sigmoid(ft[:FOX_HEADS] + bf_ref[...])


def _qkv_project(x2d, n_seq, seq_len, g_pre, w_qk, w_kvt, w_ft, b_f, tm):
    m = x2d.shape[0]
    per_seq = seq_len // tm
    row = lambda i: (i, 0)
    chan = lambda rows: pl.BlockSpec((1, rows, tm), lambda i: (i // per_seq, 0, i % per_seq))
    return pl.pallas_call(
        _qkv_kernel,
        grid=(m // tm,),
        in_specs=[pl.BlockSpec((tm, D_MODEL), row),
                  _resident((1, D_MODEL)),
                  _resident((D_MODEL, 2 * WIDTH)),
                  _resident((2 * WIDTH, D_MODEL)),
                  _resident((16, D_MODEL)),
                  _resident((FOX_HEADS, 1))],
        out_specs=[pl.BlockSpec((tm, WIDTH), row)] * 2 + [chan(WIDTH)] * 3 + [chan(FOX_HEADS)],
        out_shape=[jax.ShapeDtypeStruct((m, WIDTH), BF16)] * 2
        + [jax.ShapeDtypeStruct((n_seq, WIDTH, seq_len), BF16)]
        + [jax.ShapeDtypeStruct((n_seq, WIDTH, seq_len), F32)] * 2
        + [jax.ShapeDtypeStruct((n_seq, FOX_HEADS, seq_len), F32)],
        compiler_params=_params(1),
        name="qkv_project",
    )(x2d, g_pre, w_qk, w_kvt, w_ft, b_f)


def _neg_cumsum_kernel(lft_ref, o_ref):
    x = lft_ref[0]
    pos = lax.broadcasted_iota(jnp.int32, x.shape, 1)
    shift = 1
    while shift < x.shape[-1]:
        x = x + jnp.where(pos >= shift, pltpu.roll(x, shift, 1), 0.0)
        shift *= 2
    hi, mid, lo = _split3(-x)
    o_ref[0] = jnp.concatenate([hi.astype(F32), mid.astype(F32), lo.astype(F32)], axis=0)


def _neg_cumsum(lft):
    n_seq, heads, seq_len = lft.shape
    return pl.pallas_call(
        _neg_cumsum_kernel, grid=(n_seq,),
        in_specs=[pl.BlockSpec((1, heads, seq_len), lambda b: (b, 0, 0))],
        out_specs=pl.BlockSpec((1, 3 * heads, seq_len), lambda b: (b, 0, 0)),
        out_shape=jax.ShapeDtypeStruct((n_seq, 3 * heads, seq_len), F32),
        compiler_params=_params(1), name="decay_cumsum",
    )(lft)


def _fox_kernel(q_ref, k_ref, vt_ref, nc_ref, o_ref, m_ref, l_ref, acc_ref, s_ref, *, t):
    qi = pl.program_id(1)
    key = lax.broadcasted_iota(jnp.int32, (t, t), 0)
    qry = lax.broadcasted_iota(jnp.int32, (t, t), 1)
    causal = key <= qry
    lane = lax.broadcasted_iota(jnp.int32, (t, LANES), 1)
    lower = lane < FOX_DIM

    m_ref[...] = jnp.full(m_ref.shape, -jnp.inf, F32)
    l_ref[...] = jnp.zeros(l_ref.shape, F32)
    acc_ref[...] = jnp.zeros(acc_ref.shape, F32)

    def q_aug(h):
        lanes = slice((h // 2) * LANES, (h // 2 + 1) * LANES)
        q_blk = q_ref[0, :, lanes]
        q_h = jnp.where(lower if h % 2 == 0 else jnp.logical_not(lower), q_blk, jnp.zeros_like(q_blk))
        pick = (lane % FOX_HEADS == h) & (lane < 3 * FOX_HEADS)
        return jnp.concatenate([q_h, jnp.where(pick, 1.0, 0.0).astype(BF16)], axis=1)

    def kv_step(kv, masked):
        off = pl.multiple_of(kv * t, t)
        decay = nc_ref[0, pl.ds(off, t), :]

        def scores_to(slot, h):
            lanes = slice((h // 2) * LANES, (h // 2 + 1) * LANES)
            k_aug = jnp.concatenate([k_ref[0, pl.ds(off, t), lanes], decay], axis=1)
            s_ref[slot] = _dot_nt(k_aug, q_aug(h))

        scores_to(0, 0)
        for h in range(FOX_HEADS):
            if h + 1 < FOX_HEADS:
                scores_to((h + 1) % 2, h + 1)
            s = s_ref[h % 2]
            if masked:
                s = jnp.where(causal, s, -jnp.inf)
            m = m_ref[h:h + 1, :]
            m_new = jnp.maximum(m, jnp.max(s, axis=0, keepdims=True))
            p = jnp.exp(s - m_new)
            corr = jnp.exp(m - m_new)
            m_ref[h:h + 1, :] = m_new
            l_ref[h:h + 1, :] = l_ref[h:h + 1, :] * corr + jnp.sum(p, axis=0, keepdims=True)
            v_t = vt_ref[0, h * FOX_DIM:(h + 1) * FOX_DIM, pl.ds(off, t)]
            acc_ref[h] = acc_ref[h] * corr + _dot(v_t, p.astype(BF16))

    def body(kv, carry):
        kv_step(kv, masked=False)
        return carry

    lax.fori_loop(0, qi, body, 0)
    kv_step(qi, masked=True)

    for j in range(WIDTH // LANES):
        pair = jnp.concatenate([acc_ref[2 * j] / l_ref[2 * j:2 * j + 1, :],
                                acc_ref[2 * j + 1] / l_ref[2 * j + 1:2 * j + 2, :]], axis=0)
        o_ref[0, :, j * LANES:(j + 1) * LANES] = pair.T


def _fox_attention(q, k, v_t, decay, t):
    n_seq, seq_len, _ = q.shape
    tile = pl.BlockSpec((1, t, WIDTH), lambda b, i: (b, i, 0))
    return pl.pallas_call(
        functools.partial(_fox_kernel, t=t),
        grid=(n_seq, seq_len // t),
        in_specs=[tile,
                  pl.BlockSpec((1, seq_len, WIDTH), lambda b, i: (b, 0, 0)),
                  pl.BlockSpec((1, WIDTH, seq_len), lambda b, i: (b, 0, 0)),
                  pl.BlockSpec((1, seq_len, LANES), lambda b, i: (b, 0, 0))],
        out_specs=tile,
        out_shape=jax.ShapeDtypeStruct((n_seq, seq_len, WIDTH), F32),
        scratch_shapes=[pltpu.VMEM((FOX_HEADS, t), F32), pltpu.VMEM((FOX_HEADS, t), F32),
                        pltpu.VMEM((FOX_HEADS, FOX_DIM, t), F32), pltpu.VMEM((2, t, t), F32)],
        compiler_params=_params(2), name="fox_attention",
    )(q, k, v_t, decay)


def _mem_kv_kernel(mem_ref, g_ref, w_ref, k_ref, v_ref, kb_ref, vb_ref):
    h = _rms(mem_ref[0], g_ref[...]).astype(BF16)
    kv = _dot(h, w_ref[...])
    k_ref[0] = kv[:, :WIDTH]
    v_ref[0] = kv[:, WIDTH:]
    kb_ref[0] = kv[:, :WIDTH].astype(BF16)
    vb_ref[0] = kv[:, WIDTH:].astype(BF16)


def _mem_kv(mem, g_mem, w_mem_kv):
    n_seq = mem.shape[0]
    blk = pl.BlockSpec((1, N_MEM, WIDTH), lambda b: (b, 0, 0))
    return pl.pallas_call(
        _mem_kv_kernel, grid=(n_seq,),
        in_specs=[pl.BlockSpec((1, N_MEM, D_MODEL), lambda b: (b, 0, 0)),
                  _resident((1, D_MODEL)), _resident((D_MODEL, 2 * WIDTH))],
        out_specs=[blk] * 4,
        out_shape=[jax.ShapeDtypeStruct((n_seq, N_MEM, WIDTH), F32)] * 2
        + [jax.ShapeDtypeStruct((n_seq, N_MEM, WIDTH), BF16)] * 2,
        compiler_params=_params(1), name="mem_kv",
    )(mem, g_mem, w_mem_kv)


def _gated_merge(branch_ref, wbr_ref, wout_ref, gate_fn):
    merged = None
    for n in range(3):
        proj = _dot(branch_ref[n], wbr_ref[n])
        term = _sigmoid(gate_fn(n)) * proj
        merged = term if merged is None else merged + term
    return _dot(merged.astype(BF16), wout_ref[...])


def _finish_tile(x_ref, fox_ref, z_ref, lng_ref, lnb_ref, wsp_ref, bsp_ref,
                 mk_ref, mv_ref, wbr_ref, wout_ref, gpost_ref, y_ref, br_ref, *, tm):
    def proj(c0, width=WIDTH):
        lane0 = c0 % PROJ_CHUNK
        return z_ref[c0 // PROJ_CHUNK, :, lane0:lane0 + width]

    u = _gelu(proj(R_UA))
    v = _layer_norm(_gelu(proj(R_VA)), lng_ref[...], lnb_ref[...]).astype(BF16)
    gate_a = _silu(proj(R_ZA))
    n_chunk = tm // CHUNK
    r2 = lax.broadcasted_iota(jnp.int32, (2 * CHUNK, CHUNK), 0)
    c2 = lax.broadcasted_iota(jnp.int32, (2 * CHUNK, CHUNK), 1)
    tril = (r2 & (CHUNK - 1)) >= c2
    lower = lax.broadcasted_iota(jnp.int32, (CHUNK, LANES), 1) < (LANES // 2)
    for j in range(WIDTH // LANES):
        lanes = slice(j * LANES, (j + 1) * LANES)
        w_pair = jnp.where(tril, wsp_ref[j], 0.0).astype(BF16)
        rhs = jnp.concatenate([v[c * CHUNK:(c + 1) * CHUNK, lanes] for c in range(n_chunk)], axis=1)
        mix = _dot(w_pair, rhs)
        for c in range(n_chunk):
            blk = mix[:, c * LANES:(c + 1) * LANES]
            mixed = jnp.where(lower, blk[:CHUNK], blk[CHUNK:]) + bsp_ref[:, lanes]
            rows = slice(c * CHUNK, (c + 1) * CHUNK)
            br_ref[0, rows, lanes] = (u[rows, lanes] * mixed * gate_a[rows, lanes]).astype(BF16)

    br_ref[1] = (fox_ref[...] * _silu(proj(R_ZB))).astype(BF16)

    qc = proj(R_QC)
    gate_c = _silu(proj(R_ZC))
    for hh in range(MEM_HEADS):
        lanes = slice(hh * MEM_DIM, (hh + 1) * MEM_DIM)
        s = _dot_nt(qc[:, lanes].astype(BF16), mk_ref[0, :, lanes]) * MEM_SCALE
        p = jnp.exp(s - jnp.max(s, axis=-1, keepdims=True))
        o = _dot(p.astype(BF16), mv_ref[0, :, lanes]) / jnp.sum(p, axis=-1, keepdims=True)
        br_ref[2, :, lanes] = (o * gate_c[:, lanes]).astype(BF16)

    y = _gated_merge(br_ref, wbr_ref, wout_ref, lambda n: proj(R_GATE + n * D_MODEL, D_MODEL))
    y_ref[...] = x_ref[...] + _rms(y, gpost_ref[...])


def _norm_proj_kernel(x_ref, g_ref, w_ref, z_ref):
    z_ref[...] = _dot(_rms(x_ref[...], g_ref[...]).astype(BF16), w_ref[0])


def _norm_project(x2d, g_pre, w_chunks):
    m = x2d.shape[0]
    n_chunks, _, tn = w_chunks.shape
    return pl.pallas_call(
        _norm_proj_kernel, grid=(n_chunks,),
        in_specs=[_resident((m, D_MODEL)), _resident((1, D_MODEL)),
                  pl.BlockSpec((1, D_MODEL, tn), lambda j: (j, 0, 0))],
        out_specs=pl.BlockSpec((m, tn), lambda j: (0, j)),
        out_shape=jax.ShapeDtypeStruct((m, n_chunks * tn), F32),
        compiler_params=_params(1), name="sample_rest_project",
    )(x2d, g_pre, w_chunks)


def _decode_sequence(b, pt_ref, q_ref, kn_ref, vn_ref, cn_ref, ck_ref, cv_ref, clf_ref, o_ref,
                     ring, lfbuf, sc, acc_ref, sem_ring, sem_lf, *, n_pages, n_seq, while_values_load):
    n_grp = n_pages // GROUP

    def page_copy(src, seq, p, slot):
        return pltpu.make_async_copy(src.at[pt_ref[seq, p]], ring.at[slot], sem_ring.at[slot])

    def lf_copy(seq, p, half):
        return pltpu.make_async_copy(clf_ref.at[pt_ref[seq, p]], lfbuf.at[half, p], sem_lf.at[half])

    def start_group(src, seq, grp):
        for u in range(GROUP):
            page_copy(src, seq, grp * GROUP + u, grp * GROUP + u).start()

    def wait_group(src, grp):
        for u in range(GROUP):
            page_copy(src, b, grp * GROUP + u, grp * GROUP + u).wait()

    @pl.when(b == 0)
    def _():
        for p in range(n_pages):
            lf_copy(0, p, 0).start(priority=p % 2)
        for grp in range(n_grp):
            start_group(ck_ref, 0, grp)

    @pl.when(b + 1 < n_seq)
    def _():
        for p in range(n_pages):
            lf_copy(b + 1, p, (b + 1) % 2).start(priority=p % 2)

    half = b % 2
    for p in range(n_pages):
        lf_copy(b, p, half).wait()

    lf = lfbuf[half].reshape(n_pages * FOX_HEADS, PAGE)
    r = lax.broadcasted_iota(jnp.int32, (PAGE, PAGE), 0)
    c = lax.broadcasted_iota(jnp.int32, (PAGE, PAGE), 1)
    after = (r > c).astype(BF16)
    ones = jnp.ones((PAGE, PAGE), BF16)
    parts = _split3(lf)
    within = sum(_dot(x, after) for x in parts).reshape(n_pages, FOX_HEADS, PAGE)
    total = sum(_dot(x, ones) for x in parts).reshape(n_pages, FOX_HEADS, PAGE)
    run = jnp.broadcast_to(cn_ref[0], (FOX_HEADS, PAGE))
    for p in reversed(range(n_pages)):
        sc[p] = within[p] + run
        run = run + total[p]

    def head_rows(h):
        return slice(h * FOX_DIM, (h + 1) * FOX_DIM)

    r8 = lax.broadcasted_iota(jnp.int32, (FOX_HEADS, WIDTH), 0)
    c8 = lax.broadcasted_iota(jnp.int32, (FOX_HEADS, WIDTH), 1)
    head_mask = (c8 // FOX_DIM) == r8

    def on_lanes(col):
        return jnp.sum(jnp.where(head_mask, jnp.broadcast_to(col, head_mask.shape), 0.0), axis=0, keepdims=True)

    q_f32 = jnp.where(head_mask, jnp.broadcast_to(q_ref[0], head_mask.shape), 0.0)
    q_rows = q_f32.astype(BF16)
    acc_ref[...] = jnp.zeros((WIDTH, PAGE), F32)

    def k_group(grp, carry):
        wait_group(ck_ref, grp)
        for u in range(GROUP):
            p = grp * GROUP + u
            sc[p] = sc[p] + _dot(q_rows, ring[p].astype(BF16))
        start_group(cv_ref, b, grp)
        return carry

    lax.fori_loop(0, n_grp, k_group, 0)

    s_all = sc[...]
    s_new = jnp.sum(q_f32 * kn_ref[0], axis=-1, keepdims=True)
    m = jnp.maximum(jnp.max(jnp.max(s_all, axis=0), axis=-1, keepdims=True), s_new)
    p_all = jnp.exp(s_all - m[None])
    p_new = jnp.exp(s_new - m)
    denom = jnp.sum(jnp.sum(p_all, axis=0), axis=-1, keepdims=True) + p_new
    sc[...] = p_all

    def v_group(grp, carry):
        wait_group(cv_ref, grp)
        probs = [sc[grp * GROUP + u] for u in range(GROUP)]
        for h in range(FOX_HEADS):
            a = acc_ref[head_rows(h), :]
            for u in range(GROUP):
                a = a + ring[grp * GROUP + u, head_rows(h), :] * probs[u][h:h + 1, :]
            acc_ref[head_rows(h), :] = a

        @pl.when(b + 1 < n_seq)
        def _():
            start_group(ck_ref, b + 1, grp)

        return carry

    while_values_load()
    lax.fori_loop(0, n_grp, v_group, 0)
    out = jnp.sum(acc_ref[...].T, axis=0, keepdims=True) + on_lanes(p_new) * vn_ref[0]
    o_ref[0] = out / on_lanes(denom)


def _finish_decode_kernel(pt_ref, x_ref, fox_ref, gpre_ref, w_ref, lng_ref, lnb_ref, wsp_ref, bsp_ref,
                          mk_ref, mv_ref, wbr_ref, wout_ref, gpost_ref,
                          q_ref, kn_ref, vn_ref, cn_ref, ck_ref, cv_ref, clf_ref,
                          y_ref, o_ref,
                          br_ref, z_ref, ring, lfbuf, sc, acc_ref, sem_ring, sem_lf,
                          *, tm, n_pages, n_seq):
    def project():
        h = _rms(x_ref[...], gpre_ref[...]).astype(BF16)
        for c in range(REST_WIDTH // PROJ_CHUNK):
            z_ref[c] = _dot(h, w_ref[c])

    _decode_sequence(pl.program_id(0), pt_ref, q_ref, kn_ref, vn_ref, cn_ref, ck_ref, cv_ref, clf_ref, o_ref,
                     ring, lfbuf, sc, acc_ref, sem_ring, sem_lf, n_pages=n_pages, n_seq=n_seq,
                     while_values_load=project)
    _finish_tile(x_ref, fox_ref, z_ref, lng_ref, lnb_ref, wsp_ref, bsp_ref, mk_ref, mv_ref,
                 wbr_ref, wout_ref, gpost_ref, y_ref, br_ref, tm=tm)


def _finish_and_decode(x2d, fox2d, seq_len, g_pre, w_chunks, ln_g, ln_b, w_sp, b_sp, mkb, mvb, w_br, w_out, g_post,
                       page_table, q, k_new, v_new, c_new, cache_kt, cache_vt, cache_lft):
    m = x2d.shape[0]
    n_seq, n_pages = page_table.shape
    assert m % n_seq == 0 and seq_len % (m // n_seq) == 0 and (m // n_seq) % CHUNK == 0
    tm = m // n_seq
    assert n_pages % GROUP == 0
    assert n_pages * WIDTH * PAGE * 4 <= VMEM_LIMIT // 2
    per_seq = seq_len // tm
    n_chunks = REST_WIDTH // PROJ_CHUNK
    row = lambda i, pt: (i, 0)
    mem = pl.BlockSpec((1, N_MEM, WIDTH), lambda i, pt: (i // per_seq, 0, 0))
    tok = lambda shape: pl.BlockSpec((1,) + shape, lambda b, pt: (b, 0, 0))
    hbm = pl.BlockSpec(memory_space=pl.ANY)
    grid_spec = pltpu.PrefetchScalarGridSpec(
        num_scalar_prefetch=1, grid=(n_seq,),
        in_specs=[pl.BlockSpec((tm, D_MODEL), row), pl.BlockSpec((tm, WIDTH), row),
                  _resident((1, D_MODEL)), _resident((n_chunks, D_MODEL, PROJ_CHUNK)),
                  _resident((1, WIDTH)), _resident((1, WIDTH)),
                  _resident((WIDTH // LANES, 2 * CHUNK, CHUNK)), _resident((CHUNK, WIDTH)),
                  mem, mem,
                  _resident((3, WIDTH, D_MODEL)), _resident((D_MODEL, D_MODEL)), _resident((1, D_MODEL)),
                  tok((1, WIDTH)), tok((1, WIDTH)), tok((1, WIDTH)), tok((FOX_HEADS, 1)),
                  hbm, hbm, hbm],
        out_specs=[pl.BlockSpec((tm, D_MODEL), row), tok((1, WIDTH))],
        scratch_shapes=[pltpu.VMEM((3, tm, WIDTH), BF16),
                        pltpu.VMEM((n_chunks, tm, PROJ_CHUNK), F32),
                        pltpu.VMEM((n_pages, WIDTH, PAGE), F32),
                        pltpu.VMEM((2, n_pages, FOX_HEADS, PAGE), F32),
                        pltpu.VMEM((n_pages, FOX_HEADS, PAGE), F32),
                        pltpu.VMEM((WIDTH, PAGE), F32),
                        pltpu.SemaphoreType.DMA((n_pages,)),
                        pltpu.SemaphoreType.DMA((2,))])
    return pl.pallas_call(
        functools.partial(_finish_decode_kernel, tm=tm, n_pages=n_pages, n_seq=n_seq),
        grid_spec=grid_spec,
        out_shape=[jax.ShapeDtypeStruct((m, D_MODEL), F32), jax.ShapeDtypeStruct((n_seq, 1, WIDTH), F32)],
        compiler_params=_params(1), name="finish_and_decode",
    )(page_table, x2d, fox2d, g_pre, w_chunks, ln_g, ln_b, w_sp, b_sp, mkb, mvb, w_br, w_out, g_post,
      q, k_new, v_new, c_new, cache_kt, cache_vt, cache_lft)


def _mem_decode_kernel(q_ref, k_ref, v_ref, o_ref):
    n_tok, n_flat, _ = k_ref.shape
    sub = lax.broadcasted_iota(jnp.int32, (2 * MEM_HEADS, n_flat), 0)
    lane = lax.broadcasted_iota(jnp.int32, (2 * MEM_HEADS, n_flat), 1)
    own = (lane % MEM_HEADS) == sub
    for i in range(n_tok):
        q = q_ref[i]
        q_rows = jnp.concatenate([q, jnp.zeros_like(q)], axis=0).astype(BF16)
        s = _dot_nt(q_rows, k_ref[i].astype(BF16)) * MEM_SCALE
        m = jnp.max(jnp.where(own, s, -1e30), axis=-1, keepdims=True)
        p = jnp.where(own, jnp.exp(s - m), 0.0)
        o = _dot(p.astype(BF16), v_ref[i].astype(BF16))
        o_ref[i] = o[:MEM_HEADS] / jnp.sum(p, axis=-1, keepdims=True)[:MEM_HEADS]


def _mem_decode(q, mem_k, mem_v):
    n_seq, n_flat, _ = mem_k.shape
    n_tok = MEM_DECODE_TOKENS if n_seq % MEM_DECODE_TOKENS == 0 else 1
    tok = pl.BlockSpec((n_tok, MEM_HEADS, MEM_DIM), lambda b: (b, 0, 0))
    mem = pl.BlockSpec((n_tok, n_flat, MEM_DIM), lambda b: (b, 0, 0))
    return pl.pallas_call(
        _mem_decode_kernel, grid=(n_seq // n_tok,), in_specs=[tok, mem, mem], out_specs=tok,
        out_shape=jax.ShapeDtypeStruct((n_seq, MEM_HEADS, MEM_DIM), F32),
        compiler_params=_params(1), name="mem_decode",
    )(q, mem_k, mem_v)


def _sample_finish_kernel(x_ref, z_ref, fox_ref, mem_ref, lng_ref, lnb_ref, ws0_ref, bs0_ref,
                          wbr_ref, wout_ref, gpost_ref, y_ref, av_ref, br_ref):
    col = lambda c0, width=WIDTH: z_ref[:, c0:c0 + width]
    v = _layer_norm(_gelu(col(R_VA)), lng_ref[...], lnb_ref[...])
    av_ref[...] = v
    mixed = ws0_ref[...] * v + bs0_ref[...]
    br_ref[0] = (_gelu(col(R_UA)) * mixed * _silu(col(R_ZA))).astype(BF16)
    br_ref[1] = (fox_ref[...] * _silu(col(R_ZB))).astype(BF16)
    br_ref[2] = (mem_ref[...] * _silu(col(R_ZC))).astype(BF16)
    y = _gated_merge(br_ref, wbr_ref, wout_ref, lambda n: col(R_GATE + n * D_MODEL, D_MODEL))
    y_ref[...] = x_ref[...] + _rms(y, gpost_ref[...])


def _sample_finish(x2d, z, fox, mem, ln_g, ln_b, ws0, bs0, w_br, w_out, g_post):
    m = x2d.shape[0]
    args = (x2d, z, fox, mem, ln_g, ln_b, ws0, bs0, w_br, w_out, g_post)
    return pl.pallas_call(
        _sample_finish_kernel, grid=(1,),
        in_specs=[_resident(a.shape) for a in args],
        out_specs=[pl.BlockSpec((m, D_MODEL), lambda i: (0, 0)), pl.BlockSpec((m, WIDTH), lambda i: (0, 0))],
        out_shape=[jax.ShapeDtypeStruct((m, D_MODEL), F32), jax.ShapeDtypeStruct((m, WIDTH), F32)],
        scratch_shapes=[pltpu.VMEM((3, m, WIDTH), BF16)],
        compiler_params=_params(1), name="sample_finish",
    )(*args)


def kernel(x_prompt, x_sample, mem_prompt, cache_k, cache_v, cache_logf, cache_mem_k, cache_mem_v,
           page_table, g_pre, g_post, g_mem, w_in, b_forget, ln_v_g, ln_v_b, w_spatial, b_spatial,
           w_mem_kv, w_branch, w_out):
    depth = g_pre.shape[0]
    assert depth == 1
    n_seq, seq_len, _ = x_prompt.shape
    n_dec, dec_len, _ = x_sample.shape
    assert dec_len == 1
    n_phys = cache_k.shape[1]

    w = w_in[0]
    o_f = 7 * WIDTH
    w_qk = w[:, 3 * WIDTH:5 * WIDTH].astype(BF16)
    w_kvt = w[:, 4 * WIDTH:6 * WIDTH].T.astype(BF16)
    w_ft = jnp.zeros((16, D_MODEL), F32).at[:FOX_HEADS].set(w[:, o_f:o_f + FOX_HEADS].T).astype(BF16)
    w_rest = jnp.concatenate([w[:, :3 * WIDTH], w[:, 6 * WIDTH:o_f], w[:, o_f + FOX_HEADS:]], axis=1).astype(BF16)
    w_chunks = w_rest.reshape(D_MODEL, REST_WIDTH // PROJ_CHUNK, PROJ_CHUNK).transpose(1, 0, 2)
    b_f = b_forget[0].reshape(FOX_HEADS, 1)
    gpre, gpost, gmem = g_pre[0][None], g_post[0][None], g_mem[0][None]
    ln_g, ln_b = ln_v_g[0][None], ln_v_b[0][None]
    w_sp = w_spatial[0].reshape(WIDTH // LANES, 2 * CHUNK, CHUNK)
    b_sp = jnp.repeat(b_spatial[0].T, WIDTH // 8, axis=1)
    ws0 = jnp.repeat(w_spatial[0][:, 0, 0], WIDTH // 8)[None]
    bs0 = b_sp[:1]
    w_br = w_branch[0].astype(BF16)
    w_o = w_out[0].astype(BF16)
    w_mkv = w_mem_kv[0].astype(BF16)

    m = n_seq * seq_len
    xp = x_prompt.reshape(m, D_MODEL)
    tq = min(QKV_TILE, seq_len)
    q, kb, vtb, kt, vt, lft = _qkv_project(xp, n_seq, seq_len, gpre, w_qk, w_kvt, w_ft, b_f, tq)
    pieces = _neg_cumsum(lft)
    decay = jnp.pad(pieces.transpose(0, 2, 1), ((0, 0), (0, 0), (0, LANES - 3 * FOX_HEADS))).astype(BF16)
    t_attn = min(ATTN_TILE, seq_len)
    fox = _fox_attention(q.reshape(n_seq, seq_len, WIDTH), kb.reshape(n_seq, seq_len, WIDTH), vtb, decay, t_attn)
    mk, mv, mkb, mvb = _mem_kv(mem_prompt, gmem, w_mkv)

    xs = x_sample.reshape(n_dec, D_MODEL)
    qs, kbs, _, kts, vts, lfts = _qkv_project(xs, 1, n_dec, gpre, w_qk, w_kvt, w_ft, b_f, n_dec)
    logf_s = lfts[0].T
    zs = _norm_project(xs, gpre, w_chunks)

    tok = lambda a: a.astype(F32).reshape(n_dec, 1, WIDTH)
    pages_t = lambda c: c[0].transpose(0, 2, 3, 1).reshape(n_phys, WIDTH, PAGE)
    yp, fox_s = _finish_and_decode(
        xp, fox.reshape(m, WIDTH), seq_len, gpre, w_chunks, ln_g, ln_b, w_sp, b_sp, mkb, mvb, w_br, w_o, gpost,
        page_table, tok(qs), tok(kbs), tok(vts[0].T), logf_s.reshape(n_dec, FOX_HEADS, 1),
        pages_t(cache_k), pages_t(cache_v), cache_logf[0].transpose(0, 2, 1))
    mem_rows = lambda c: c.reshape(n_dec, N_MEM * MEM_HEADS, MEM_DIM)
    mem_s = _mem_decode(zs[:, R_QC:R_QC + WIDTH].reshape(n_dec, MEM_HEADS, MEM_DIM),
                        mem_rows(cache_mem_k), mem_rows(cache_mem_v))
    ys, av = _sample_finish(xs, zs, fox_s.reshape(n_dec, WIDTH), mem_s.reshape(n_dec, WIDTH),
                            ln_g, ln_b, ws0, bs0, w_br, w_o, gpost)

    def token_major(t, n, length):
        return t.reshape(1, n, FOX_HEADS, FOX_DIM, length).transpose(0, 1, 4, 2, 3)

    return (yp.reshape(n_seq, seq_len, D_MODEL),
            ys.reshape(n_dec, 1, D_MODEL),
            token_major(kt, n_seq, seq_len), token_major(vt, n_seq, seq_len),
            lft.transpose(0, 2, 1)[None],
            mk.reshape(1, n_seq, N_MEM, MEM_HEADS, MEM_DIM), mv.reshape(1, n_seq, N_MEM, MEM_HEADS, MEM_DIM),
            token_major(kts, 1, n_dec).reshape(1, n_dec, 1, FOX_HEADS, FOX_DIM),
            token_major(vts, 1, n_dec).reshape(1, n_dec, 1, FOX_HEADS, FOX_DIM),
            logf_s.reshape(1, n_dec, 1, FOX_HEADS),
            av.reshape(1, n_dec, 1, WIDTH))
```
